```python
import jax, jax.numpy as jnp
from jax import lax
import numpy as np

D_MODEL = 2048
BATCH = 1
SEQ = 16384
DEPTH = 1

CTX_LEN = 256
GRID_W = 64
N_HEADS = 8
N_KV_HEADS = 2
HEAD_DIM = 128
GROUP = N_HEADS // N_KV_HEADS
ATTN_WIDTH = N_HEADS * HEAD_DIM
KV_WIDTH = N_KV_HEADS * HEAD_DIM
CONV_WIDTH = D_MODEL - ATTN_WIDTH
CONV_GROUPS = 16
CONV_K = 3
IN_WIDTH = ATTN_WIDTH + 2 * KV_WIDTH + 3 * CONV_WIDTH
D_FF = 5632
ROPE_THETA = 10000.0
ROPE_AXIS_DIM = HEAD_DIM // 2
Q_BLOCK = 128
EPS = 1e-6
N_MOD = 6

kernel_name = "hybrid_gqa_shortconv_convffn_dit"


def rmsnorm(x, w):
    xf = x.astype(jnp.float32)
    y = xf * lax.rsqrt(jnp.mean(xf * xf, axis=-1, keepdims=True) + EPS)
    return (y * w.astype(jnp.float32)).astype(x.dtype)


def modulate(h, shift, scale):
    return h * (1.0 + scale[:, None, :]) + shift[:, None, :]


def dwconv3(x, w):
    xp = jnp.pad(x, ((0, 0), (1, 1), (0, 0)))
    return xp[:, :-2] * w[0] + xp[:, 1:-1] * w[1] + xp[:, 2:] * w[2]


def rope_axis(x, pos):
    half = ROPE_AXIS_DIM // 2
    freqs = ROPE_THETA ** (-jnp.arange(half, dtype=jnp.float32) / half)
    ang = pos.astype(jnp.float32)[:, None] * freqs[None, :]
    cos = jnp.cos(ang)[None, :, None, :]
    sin = jnp.sin(ang)[None, :, None, :]
    xf = x.astype(jnp.float32)
    x1, x2 = xf[..., :half], xf[..., half:]
    return jnp.concatenate([x1 * cos - x2 * sin, x2 * cos + x1 * sin], axis=-1)


def rope_2d(x, row, col):
    out = jnp.concatenate([rope_axis(x[..., :ROPE_AXIS_DIM], row),
                           rope_axis(x[..., ROPE_AXIS_DIM:], col)], axis=-1)
    return out.astype(x.dtype)


def split_proj(p):
    o = 0
    q = p[..., o:o + ATTN_WIDTH]; o += ATTN_WIDTH
    k = p[..., o:o + KV_WIDTH]; o += KV_WIDTH
    v = p[..., o:o + KV_WIDTH]; o += KV_WIDTH
    cb = p[..., o:o + CONV_WIDTH]; o += CONV_WIDTH
    cc = p[..., o:o + CONV_WIDTH]; o += CONV_WIDTH
    ch = p[..., o:o + CONV_WIDTH]
    return q, k, v, cb, cc, ch


def heads(t, n_heads):
    b, n = t.shape[:2]
    return t.reshape(b, n, n_heads, HEAD_DIM)


def attend(qb, k, v):
    b, nq = qb.shape[:2]
    qg = qb.reshape(b, nq, N_KV_HEADS, GROUP, HEAD_DIM)
    s = jnp.einsum('bqkgd,bskd->bkgqs', qg, k,
                   preferred_element_type=jnp.float32) * (HEAD_DIM ** -0.5)
    p = jax.nn.softmax(s, axis=-1).astype(v.dtype)
    o = jnp.einsum('bkgqs,bskd->bqkgd', p, v)
    return o.reshape(b, nq, ATTN_WIDTH)


def short_conv(cb, cc, ch, w):
    return cb * dwconv3(cc * ch, w)


def merge_groups(attn, conv, aon, con, w_o):
    return jnp.concatenate([rmsnorm(attn, aon), rmsnorm(conv, con)], axis=-1) @ w_o


def conv_ffn(h, w_up, w_conv, w_down):
    u = dwconv3(h @ w_up, w_conv)
    a, g = u[..., :D_FF], u[..., D_FF:]
    return (jax.nn.silu(g) * a) @ w_down


def setup_inputs(seed: int = 0) -> dict:
    key = jax.random.key(seed)
    ks = jax.random.split(key, 20)
    f32 = jnp.float32
    nrm = lambda k, s, sc: jax.random.normal(k, s, f32) * sc
    return {
        "x": nrm(ks[0], (BATCH, SEQ, D_MODEL), 1.0),
        "c": nrm(ks[1], (BATCH, D_MODEL), 1.0),
        "ctx": nrm(ks[2], (BATCH, CTX_LEN, D_MODEL), 1.0),
        "c_ctx": nrm(ks[3], (D_MODEL,), 1.0),
        "w_ada": nrm(ks[4], (DEPTH, D_MODEL, N_MOD * D_MODEL), 0.5 * D_MODEL ** -0.5),
        "b_ada": nrm(ks[5], (DEPTH, N_MOD * D_MODEL), 0.01),
        "norm1_w": 1.0 + nrm(ks[6], (DEPTH, D_MODEL), 0.02),
        "w_in": nrm(ks[7], (DEPTH, D_MODEL, IN_WIDTH), D_MODEL ** -0.5),
        "q_norm_w": 1.0 + nrm(ks[8], (DEPTH, HEAD_DIM), 0.02),
        "k_norm_w": 1.0 + nrm(ks[9], (DEPTH, HEAD_DIM), 0.02),
        "conv_w": nrm(ks[10], (DEPTH, CONV_K, CONV_WIDTH), CONV_K ** -0.5),
        "attn_out_norm_w": 1.0 + nrm(ks[11], (DEPTH, ATTN_WIDTH), 0.02),
        "conv_out_norm_w": 1.0 + nrm(ks[12], (DEPTH, CONV_WIDTH), 0.02),
        "w_o": nrm(ks[13], (DEPTH, D_MODEL, D_MODEL), D_MODEL ** -0.5),
        "norm2_w": 1.0 + nrm(ks[14], (DEPTH, D_MODEL), 0.02),
        "w_ffn_up": nrm(ks[15], (DEPTH, D_MODEL, 2 * D_FF), D_MODEL ** -0.5),
        "ffn_conv_w": nrm(ks[16], (DEPTH, CONV_K, 2 * D_FF), CONV_K ** -0.5),
        "w_ffn_down": nrm(ks[17], (DEPTH, D_FF, D_MODEL), D_FF ** -0.5),
        "final_norm_w": 1.0 + nrm(ks[18], (D_MODEL,), 0.02),
    }


def reference(x, c, ctx, c_ctx, w_ada, b_ada, norm1_w, w_in, q_norm_w, k_norm_w,
              conv_w, attn_out_norm_w, conv_out_norm_w, w_o, norm2_w,
              w_ffn_up, ffn_conv_w, w_ffn_down, final_norm_w):
    b, n, _ = x.shape
    rows = n // GRID_W
    row = jnp.repeat(jnp.arange(rows, dtype=jnp.int32), GRID_W)
    col = jnp.tile(jnp.arange(GRID_W, dtype=jnp.int32), rows)
    nb = n // Q_BLOCK

    xs, cs = x, ctx
    for i in range(DEPTH):
        last = i == DEPTH - 1
        mod = jax.nn.silu(c) @ w_ada[i] + b_ada[i]
        mod_c = jax.nn.silu(c_ctx)[None, :] @ w_ada[i] + b_ada[i]
        sh1, sc1, g1, sh2, sc2, g2 = jnp.split(mod, N_MOD, axis=-1)
        csh1, csc1, cg1, csh2, csc2, cg2 = jnp.split(mod_c, N_MOD, axis=-1)

        h = modulate(rmsnorm(xs, norm1_w[i]), sh1, sc1)
        hc = modulate(rmsnorm(cs, norm1_w[i]), csh1, csc1)
        q, k, v, cb, cc, ch = split_proj(h @ w_in[i])
        qc, kc, vc, cbc, ccc, chc = split_proj(hc @ w_in[i])

        q = rope_2d(rmsnorm(heads(q, N_HEADS), q_norm_w[i]), row, col)
        k = rope_2d(rmsnorm(heads(k, N_KV_HEADS), k_norm_w[i]), row, col)
        v = heads(v, N_KV_HEADS)
        kc = rmsnorm(heads(kc, N_KV_HEADS), k_norm_w[i])
        vc = heads(vc, N_KV_HEADS)
        k_all = jnp.concatenate([kc, k], axis=1)
        v_all = jnp.concatenate([vc, v], axis=1)

        q_blocks = q.reshape(b, nb, Q_BLOCK, N_HEADS, HEAD_DIM).swapaxes(0, 1)
        attn = lax.map(lambda qb: attend(qb, k_all, v_all), q_blocks)
        attn = attn.swapaxes(0, 1).reshape(b, n, ATTN_WIDTH)
        conv = short_conv(cb, cc, ch, conv_w[i])
        xs = xs + g1[:, None, :] * merge_groups(attn, conv, attn_out_norm_w[i],
                                                  conv_out_norm_w[i], w_o[i])

        if not last:
            qc = rmsnorm(heads(qc, N_HEADS), q_norm_w[i])
            attn_c = attend(qc, kc, vc)
            conv_c = short_conv(cbc, ccc, chc, conv_w[i])
            cs = cs + cg1[:, None, :] * merge_groups(attn_c, conv_c, attn_out_norm_w[i],
                                                       conv_out_norm_w[i], w_o[i])

        h2 = modulate(rmsnorm(xs, norm2_w[i]), sh2, sc2)
        xs = xs + g2[:, None, :] * conv_ffn(h2, w_ffn_up[i], ffn_conv_w[i], w_ffn_down[i])
        if not last:
            hc2 = modulate(rmsnorm(cs, norm2_w[i]), csh2, csc2)
            cs = cs + cg2[:, None, :] * conv_ffn(hc2, w_ffn_up[i], ffn_conv_w[i], w_ffn_down[i])

    return rmsnorm(xs, final_norm_w)
```

```python
import functools
import math

import jax
import jax.numpy as jnp
from jax import lax
from jax.experimental import pallas as pl
from jax.experimental.pallas import tpu as pltpu

GRID_W = 64
N_HEADS = 8
N_KV_HEADS = 2
HEAD_DIM = 128
GROUP = N_HEADS // N_KV_HEADS
ROPE_THETA = 10000.0
ROPE_AXIS_DIM = HEAD_DIM // 2
EPS = 1e-6
N_MOD = 6
CONV_K = 3

LANES = 128
SUBLANES_F32 = 8
VMEM_LIMIT_BYTES = 56 * 1024 * 1024

ADA_TN = 1024
ROPE_TR = 2048
TM_IN = 512
TQ = 256
TK = 512
TM_MERGE = 512
TM_FFN = 512
TF_FFN = 512
HALO = SUBLANES_F32

_BF16 = jnp.bfloat16
_F32 = jnp.float32


def _params(n_axes):
    return pltpu.CompilerParams(dimension_semantics=("arbitrary",) * n_axes,
                                vmem_limit_bytes=VMEM_LIMIT_BYTES)


def _rms(x, w):
    return x * lax.rsqrt(jnp.mean(x * x, axis=-1, keepdims=True) + EPS) * w


def _shift_rows(u, tm):
    rows = u.shape[0]
    prev = pltpu.roll(u, 1, axis=0)[HALO:HALO + tm]
    nxt = pltpu.roll(u, rows - 1, axis=0)[HALO:HALO + tm]
    return prev, u[HALO:HALO + tm], nxt


def _halo_valid(i, tm, n_rows):
    t = i * tm - HALO + lax.broadcasted_iota(jnp.int32, (tm + 2 * HALO, 1), 0)
    return (t >= 0) & (t < n_rows)


def _adaln_kernel(cb_ref, w_ref, b_ref, o_ref):
    rows = []
    for r in range(2):
        cv = cb_ref[r]
        s = cv / (1.0 + jnp.exp(-cv))
        parts = []
        for j in range(ADA_TN // LANES):
            wj = w_ref[:, j * LANES:(j + 1) * LANES]
            parts.append(jnp.sum(s * wj, axis=0, keepdims=True))
        rows.append(jnp.concatenate(parts, axis=1) + b_ref[...])
    pad = jnp.zeros((SUBLANES_F32 - 2, ADA_TN), _F32)
    o_ref[...] = jnp.concatenate(rows + [pad], axis=0)


def _adaln(c, c_ctx, w_ada, b_ada):
    d = w_ada.shape[0]
    n_out = w_ada.shape[1]
    cvec = jnp.stack([c[0], c_ctx], axis=0)
    cb = jnp.broadcast_to(cvec[:, :, None], (2, d, LANES))
    return pl.pallas_call(
        _adaln_kernel,
        grid=(n_out // ADA_TN,),
        in_specs=[pl.BlockSpec((2, d, LANES), lambda j: (0, 0, 0)),
                  pl.BlockSpec((d, ADA_TN), lambda j: (0, j)),
                  pl.BlockSpec((1, ADA_TN), lambda j: (0, j))],
        out_specs=pl.BlockSpec((SUBLANES_F32, ADA_TN), lambda j: (0, j)),
        out_shape=jax.ShapeDtypeStruct((SUBLANES_F32, n_out), _F32),
        compiler_params=_params(1),
        name="adaln",
    )(cb, w_ada, b_ada.reshape(1, n_out))


def _rope_kernel(freq_ref, sign_ref, cos_ref, sin_ref):
    i = pl.program_id(0)
    t = i * ROPE_TR + lax.broadcasted_iota(jnp.int32, (ROPE_TR, LANES), 0)
    lane = lax.broadcasted_iota(jnp.int32, (ROPE_TR, LANES), 1)
    shift = int(math.log2(GRID_W))
    pos = jnp.where(lane < ROPE_AXIS_DIM, lax.shift_right_logical(t, shift), t & (GRID_W - 1))
    ang = pos.astype(_F32) * freq_ref[...]
    cos_ref[...] = jnp.cos(ang)
    sin_ref[...] = jnp.sin(ang) * sign_ref[...]


def _rope_tables(n):
    assert GRID_W & (GRID_W - 1) == 0
    half = ROPE_AXIS_DIM // 2
    freqs = ROPE_THETA ** (-jnp.arange(half, dtype=_F32) / half)
    freq = jnp.tile(freqs, HEAD_DIM // half).reshape(1, HEAD_DIM)
    sign = jnp.tile(jnp.concatenate([-jnp.ones((half,), _F32), jnp.ones((half,), _F32)]),
                    HEAD_DIM // ROPE_AXIS_DIM).reshape(1, HEAD_DIM)
    tr = min(ROPE_TR, n)
    assert n % tr == 0 and tr == ROPE_TR
    vec = pl.BlockSpec((1, HEAD_DIM), lambda i: (0, 0))
    tab = pl.BlockSpec((ROPE_TR, HEAD_DIM), lambda i: (i, 0))
    return pl.pallas_call(
        _rope_kernel,
        grid=(n // ROPE_TR,),
        in_specs=[vec, vec],
        out_specs=[tab, tab],
        out_shape=[jax.ShapeDtypeStruct((n, HEAD_DIM), _F32)] * 2,
        compiler_params=_params(1),
        name="rope_tables",
    )(freq, sign)


def _rope_apply(x, cos, sin_signed):
    half = ROPE_AXIS_DIM // 2
    lane = lax.broadcasted_iota(jnp.int32, x.shape, 1)
    first = (lane & (ROPE_AXIS_DIM - 1)) < half
    swapped = jnp.where(first, pltpu.roll(x, HEAD_DIM - half, axis=1), pltpu.roll(x, half, axis=1))
    return x * cos + swapped * sin_signed


def _ctx_kv_kernel(x_ref, n1_ref, sh_ref, sc_ref, w_ref, kn_ref, k_ref, v_ref):
    x = x_ref[...]
    h = _rms(x, n1_ref[...]) * (1.0 + sc_ref[1:2, :]) + sh_ref[1:2, :]
    p = jnp.dot(h.astype(_BF16), w_ref[...], preferred_element_type=_F32)
    kvw = N_KV_HEADS * HEAD_DIM
    for g in range(N_KV_HEADS):
        kg = p[:, g * HEAD_DIM:(g + 1) * HEAD_DIM]
        k_ref[:, g * HEAD_DIM:(g + 1) * HEAD_DIM] = _rms(kg, kn_ref[...]).astype(_BF16)
    v_ref[...] = p[:, kvw:].astype(_BF16)


def _ctx_kv(ctx2d, norm1_w, mod, w_in_b, k_norm_w):
    c_len, d = ctx2d.shape
    attn_w = N_HEADS * HEAD_DIM
    kvw = N_KV_HEADS * HEAD_DIM
    assert attn_w % (2 * kvw) == 0
    return pl.pallas_call(
        _ctx_kv_kernel,
        grid=(1,),
        in_specs=[pl.BlockSpec((c_len, d), lambda i: (0, 0)),
                  pl.BlockSpec((1, d), lambda i: (0, 0)),
                  pl.BlockSpec((SUBLANES_F32, d), lambda i: (0, 0)),
                  pl.BlockSpec((SUBLANES_F32, d), lambda i: (0, 1)),
                  pl.BlockSpec((d, 2 * kvw), lambda i: (0, attn_w // (2 * kvw))),
                  pl.BlockSpec((1, HEAD_DIM), lambda i: (0, 0))],
        out_specs=[pl.BlockSpec((c_len, kvw), lambda i: (0, 0))] * 2,
        out_shape=[jax.ShapeDtypeStruct((c_len, kvw), _BF16)] * 2,
        compiler_params=_params(1),
        name="ctx_kv",
    )(ctx2d, norm1_w, mod, mod, w_in_b, k_norm_w)


def _in_proj_kernel(n_rows, xp_ref, xm_ref, xn_ref, n1_ref, sh_ref, sc_ref, w_ref, qn_ref, kn_ref,
                    cos_ref, sin_ref, cw_ref, con_ref, q_ref, k_ref, v_ref, conv_ref):
    i = pl.program_id(0)
    tm = xm_ref.shape[0]
    attn_w = N_HEADS * HEAD_DIM
    kvw = N_KV_HEADS * HEAD_DIM
    conv_w = conv_ref.shape[1]

    xe = jnp.concatenate([xp_ref[...], xm_ref[...], xn_ref[...]], axis=0)
    h = _rms(xe, n1_ref[...]) * (1.0 + sc_ref[0:1, :]) + sh_ref[0:1, :]
    hb_ext = h.astype(_BF16)
    hb = hb_ext[HALO:HALO + tm]
    cos = cos_ref[...]
    sin = sin_ref[...]

    def proj(lhs, lo, width):
        return jnp.dot(lhs, w_ref[:, lo:lo + width], preferred_element_type=_F32)

    step = 2 * HEAD_DIM
    for c0 in range(0, attn_w, step):
        pq = proj(hb, c0, step)
        for s in range(0, step, HEAD_DIM):
            qh = _rms(pq[:, s:s + HEAD_DIM], qn_ref[...])
            q_ref[:, c0 + s:c0 + s + HEAD_DIM] = _rope_apply(qh, cos, sin).astype(_BF16)
    pk = proj(hb, attn_w, kvw)
    for s in range(0, kvw, HEAD_DIM):
        kh = _rms(pk[:, s:s + HEAD_DIM], kn_ref[...])
        k_ref[:, s:s + HEAD_DIM] = _rope_apply(kh, cos, sin).astype(_BF16)
    v_ref[...] = proj(hb, attn_w + kvw, kvw).astype(_BF16)

    o = attn_w + 2 * kvw
    valid = _halo_valid(i, tm, n_rows)
    cb = proj(hb, o, conv_w)
    u = proj(hb_ext, o + conv_w, conv_w) * proj(hb_ext, o + 2 * conv_w, conv_w)
    u = jnp.where(valid, u, 0.0)
    up, uc, un = _shift_rows(u, tm)
    y = cb * (up * cw_ref[0:1, :] + uc * cw_ref[1:2, :] + un * cw_ref[2:3, :])
    conv_ref[...] = _rms(y, con_ref[...]).astype(_BF16)


def _in_proj(x2d, norm1_w, mod, w_in_b, qn_scaled, k_norm_w, cos, sin, conv_w, con_w):
    n, d = x2d.shape
    tm = TM_IN
    assert n % tm == 0 and tm % HALO == 0
    attn_w = N_HEADS * HEAD_DIM
    kvw = N_KV_HEADS * HEAD_DIM
    cw = conv_w.shape[1]
    r = tm // HALO
    last_halo = n // HALO - 1
    vec = lambda width: pl.BlockSpec((1, width), lambda i: (0, 0))
    row = lambda width: pl.BlockSpec((tm, width), lambda i: (i, 0))
    return pl.pallas_call(
        functools.partial(_in_proj_kernel, n),
        grid=(n // tm,),
        in_specs=[pl.BlockSpec((HALO, d), lambda i: (jnp.maximum(i * r - 1, 0), 0)),
                  row(d),
                  pl.BlockSpec((HALO, d), lambda i: (jnp.minimum((i + 1) * r, last_halo), 0)),
                  vec(d),
                  pl.BlockSpec((SUBLANES_F32, d), lambda i: (0, 0)),
                  pl.BlockSpec((SUBLANES_F32, d), lambda i: (0, 1)),
                  pl.BlockSpec(w_in_b.shape, lambda i: (0, 0), pipeline_mode=pl.Buffered(1)),
                  vec(HEAD_DIM), vec(HEAD_DIM),
                  row(HEAD_DIM), row(HEAD_DIM),
                  pl.BlockSpec((CONV_K, cw), lambda i: (0, 0)),
                  vec(cw)],
        out_specs=[row(attn_w), row(kvw), row(kvw), row(cw)],
        out_shape=[jax.ShapeDtypeStruct((n, attn_w), _BF16),
                   jax.ShapeDtypeStruct((n, kvw), _BF16),
                   jax.ShapeDtypeStruct((n, kvw), _BF16),
                   jax.ShapeDtypeStruct((n, cw), _BF16)],
        compiler_params=_params(1),
        name="in_proj",
    )(x2d, x2d, x2d, norm1_w, mod, mod, w_in_b, qn_scaled, k_norm_w, cos, sin, conv_w, con_w)


def _attn_kernel(q_ref, kc_ref, vc_ref, k_ref, v_ref, aon_ref, o_ref, m_ref, l_ref, acc_ref, out_ref):
    tq = q_ref.shape[0]
    n_keys = k_ref.shape[0]
    rows = GROUP * tq

    def block(qg, kb, vb):
        s = lax.dot_general(qg, kb, (((1,), (1,)), ((), ())), preferred_element_type=_F32)
        m_old = m_ref[...]
        m_new = jnp.maximum(m_old, jnp.max(s, axis=-1, keepdims=True))
        alpha = jnp.exp2(m_old - m_new)
        p = jnp.exp2(s - m_new)
        l_ref[...] = alpha * l_ref[...] + jnp.sum(p, axis=-1, keepdims=True)
        acc_ref[...] = alpha * acc_ref[...] + jnp.dot(p.astype(_BF16), vb, preferred_element_type=_F32)
        m_ref[...] = m_new

    for g in range(N_KV_HEADS):
        lo = g * HEAD_DIM
        qg = jnp.concatenate(
            [q_ref[:, (g * GROUP + j) * HEAD_DIM:(g * GROUP + j + 1) * HEAD_DIM] for j in range(GROUP)],
            axis=0)
        m_ref[...] = jnp.full((rows, 1), -jnp.inf, _F32)
        l_ref[...] = jnp.zeros((rows, 1), _F32)
        acc_ref[...] = jnp.zeros((rows, HEAD_DIM), _F32)
        block(qg, kc_ref[:, lo:lo + HEAD_DIM], vc_ref[:, lo:lo + HEAD_DIM])

        def body(kb_i, carry):
            start = pl.multiple_of(kb_i * TK, TK)
            block(qg, k_ref[pl.ds(start, TK), lo:lo + HEAD_DIM], v_ref[pl.ds(start, TK), lo:lo + HEAD_DIM])
            return carry

        lax.fori_loop(0, n_keys // TK, body, 0)
        og = acc_ref[...] / l_ref[...]
        for j in range(GROUP):
            hcol = (g * GROUP + j) * HEAD_DIM
            out_ref[:, hcol:hcol + HEAD_DIM] = og[j * tq:(j + 1) * tq]
    o_ref[...] = _rms(out_ref[...], aon_ref[...]).astype(_BF16)


def _attention(q, kc, vc, k, v, aon_w):
    n, attn_w = q.shape
    kvw = k.shape[1]
    c_len = kc.shape[0]
    assert n % TQ == 0 and n % TK == 0
    full = lambda a: pl.BlockSpec(a.shape, lambda i: (0, 0), pipeline_mode=pl.Buffered(1))
    return pl.pallas_call(
        _attn_kernel,
        grid=(n // TQ,),
        in_specs=[pl.BlockSpec((TQ, attn_w), lambda i: (i, 0)),
                  full(kc), full(vc), full(k), full(v),
                  pl.BlockSpec((1, attn_w), lambda i: (0, 0))],
        out_specs=pl.BlockSpec((TQ, attn_w), lambda i: (i, 0)),
        out_shape=jax.ShapeDtypeStruct((n, attn_w), _BF16),
        scratch_shapes=[pltpu.VMEM((GROUP * TQ, 1), _F32),
                        pltpu.VMEM((GROUP * TQ, 1), _F32),
                        pltpu.VMEM((GROUP * TQ, HEAD_DIM), _F32),
                        pltpu.VMEM((TQ, attn_w), _F32)],
        compiler_params=_params(1),
        name="attention",
    )(q, kc, vc, k, v, aon_w)


def _merge_kernel(a_ref, c_ref, w_ref, x_ref, g_ref, o_ref):
    aw = a_ref.shape[1]
    y = jnp.dot(a_ref[...], w_ref[0:aw, :], preferred_element_type=_F32)
    y = y + jnp.dot(c_ref[...], w_ref[aw:, :], preferred_element_type=_F32)
    o_ref[...] = x_ref[...] + g_ref[0:1, :] * y


def _merge(attn_n, conv_n, w_o_b, x2d, mod):
    n, d = x2d.shape
    tm = TM_MERGE
    assert n % tm == 0
    row = lambda width: pl.BlockSpec((tm, width), lambda i: (i, 0))
    return pl.pallas_call(
        _merge_kernel,
        grid=(n // tm,),
        in_specs=[row(attn_n.shape[1]), row(conv_n.shape[1]),
                  pl.BlockSpec(w_o_b.shape, lambda i: (0, 0), pipeline_mode=pl.Buffered(1)),
                  row(d),
                  pl.BlockSpec((SUBLANES_F32, d), lambda i: (0, 2))],
        out_specs=row(d),
        out_shape=jax.ShapeDtypeStruct((n, d), _F32),
        compiler_params=_params(1),
        name="merge",
    )(attn_n, conv_n, w_o_b, x2d, mod)


def _ffn_kernel(n_rows, xp_ref, xm_ref, xn_ref, n2_ref, sh_ref, sc_ref, g_ref, wa_ref, wg_ref,
                cwa_ref, cwg_ref, wd_ref, fn_ref, o_ref, h_ref, acc_ref):
    i = pl.program_id(0)
    j = pl.program_id(1)
    tm = xm_ref.shape[0]

    @pl.when(j == 0)
    def _():
        xe = jnp.concatenate([xp_ref[...], xm_ref[...], xn_ref[...]], axis=0)
        h = _rms(xe, n2_ref[...]) * (1.0 + sc_ref[0:1, :]) + sh_ref[0:1, :]
        h_ref[...] = jnp.where(_halo_valid(i, tm, n_rows), h, 0.0).astype(_BF16)
        acc_ref[...] = jnp.zeros_like(acc_ref)

    hb = h_ref[...]

    def conv(w_ref, cw_ref):
        u = jnp.dot(hb, w_ref[...], preferred_element_type=_F32)
        up, uc, un = _shift_rows(u, tm)
        return up * cw_ref[0:1, :] + uc * cw_ref[1:2, :] + un * cw_ref[2:3, :]

    a = conv(wa_ref, cwa_ref)
    g = conv(wg_ref, cwg_ref)
    act = (g / (1.0 + jnp.exp(-g))) * a
    acc_ref[...] += jnp.dot(act.astype(_BF16), wd_ref[...], preferred_element_type=_F32)

    @pl.when(j == pl.num_programs(1) - 1)
    def _():
        xs = xm_ref[...] + g_ref[0:1, :] * acc_ref[...]
        o_ref[...] = _rms(xs, fn_ref[...])


def _conv_ffn(xs, norm2_w, mod, w_up_b, ffn_conv_w, w_down_b, final_w):
    n, d = xs.shape
    d_ff = w_down_b.shape[0]
    tm, tf = TM_FFN, TF_FFN
    assert n % tm == 0 and d_ff % tf == 0 and tm % HALO == 0
    nf = d_ff // tf
    r = tm // HALO
    last_halo = n // HALO - 1
    vec = pl.BlockSpec((1, d), lambda i, j: (0, 0))
    modspec = lambda k: pl.BlockSpec((SUBLANES_F32, d), lambda i, j: (0, k))
    return pl.pallas_call(
        functools.partial(_ffn_kernel, n),
        grid=(n // tm, nf),
        in_specs=[pl.BlockSpec((HALO, d), lambda i, j: (jnp.maximum(i * r - 1, 0), 0)),
                  pl.BlockSpec((tm, d), lambda i, j: (i, 0)),
                  pl.BlockSpec((HALO, d), lambda i, j: (jnp.minimum((i + 1) * r, last_halo), 0)),
                  vec, modspec(3), modspec(4), modspec(5),
                  pl.BlockSpec((d, tf), lambda i, j: (0, j)),
                  pl.BlockSpec((d, tf), lambda i, j: (0, j + nf)),
                  pl.BlockSpec((CONV_K, tf), lambda i, j: (0, j)),
                  pl.BlockSpec((CONV_K, tf), lambda i, j: (0, j + nf)),
                  pl.BlockSpec((tf, d), lambda i, j: (j, 0)),
                  vec],
        out_specs=pl.BlockSpec((tm, d), lambda i, j: (i, 0)),
        out_shape=jax.ShapeDtypeStruct((n, d), _F32),
        scratch_shapes=[pltpu.VMEM((tm + 2 * HALO, d), _BF16),
                        pltpu.VMEM((tm, d), _F32)],
        compiler_params=_params(2),
        name="conv_ffn",
    )(xs, xs, xs, norm2_w, mod, mod, mod, w_up_b, w_up_b, ffn_conv_w, ffn_conv_w, w_down_b, final_w)


def kernel(x, c, ctx, c_ctx, w_ada, b_ada, norm1_w, w_in, q_norm_w, k_norm_w, conv_w, attn_out_norm_w,
           conv_out_norm_w, w_o, norm2_w, w_ffn_up, ffn_conv_w, w_ffn_down, final_norm_w):
    b, n, d = x.shape
    assert b == 1 and w_ada.shape[0] == 1, "single batch element, single trunk layer"
    x2d = x[0]
    row = lambda v: v.reshape(1, -1)

    mod = _adaln(c, c_ctx, w_ada[0], b_ada[0])
    cos, sin = _rope_tables(n)

    w_in_b = w_in[0].astype(_BF16)
    qn_scaled = row(q_norm_w[0]) * (HEAD_DIM ** -0.5 * math.log2(math.e))
    kc, vc = _ctx_kv(ctx[0], row(norm1_w[0]), mod, w_in_b, row(k_norm_w[0]))
    q, k, v, conv_n = _in_proj(x2d, row(norm1_w[0]), mod, w_in_b, qn_scaled, row(k_norm_w[0]),
                               cos, sin, conv_w[0], row(conv_out_norm_w[0]))
    attn_n = _attention(q, kc, vc, k, v, row(attn_out_norm_w[0]))
    xs = _merge(attn_n, conv_n, w_o[0].astype(_BF16), x2d, mod)
    out = _conv_ffn(xs, row(norm2_w[0]), mod, w_ffn_up[0].astype(_BF16), ffn_conv_w[0],
                    w_ffn_down[0].astype(_BF16), row(final_norm_w))
    return out[None]
```

```python
import functools
import math

import jax
import jax.numpy as jnp
from jax import lax
from jax.experimental import pallas as pl
from jax.experimental.pallas import tpu as pltpu

GRID_W = 64
N_HEADS = 8
N_KV_HEADS = 2
HEAD_DIM = 128
GROUP = N_HEADS // N_KV_HEADS
ROPE_THETA = 10000.0
ROPE_AXIS_DIM = HEAD_DIM // 2
EPS = 1e-6
N_MOD = 6
CONV_K = 3

LANES = 128
SUBLANES_F32 = 8
VMEM_LIMIT_BYTES = 56 * 1024 * 1024

ADA_TN = 1024
ROPE_TR = 2048
TM_IN = 512
TQ = 256
TK = 1024
TM_MERGE = 512
TM_FFN = 512
TF_FFN = 512
HALO = SUBLANES_F32

_BF16 = jnp.bfloat16
_F32 = jnp.float32


def _params(n_axes):
    return pltpu.CompilerParams(dimension_semantics=("arbitrary",) * n_axes,
                                vmem_limit_bytes=VMEM_LIMIT_BYTES)


def _rms(x, w):
    return x * lax.rsqrt(jnp.mean(x * x, axis=-1, keepdims=True) + EPS) * w


def _shift_rows(u, tm):
    rows = u.shape[0]
    prev = pltpu.roll(u, 1, axis=0)[HALO:HALO + tm]
    nxt = pltpu.roll(u, rows - 1, axis=0)[HALO:HALO + tm]
    return prev, u[HALO:HALO + tm], nxt


def _halo_valid(i, tm, n_rows):
    t = i * tm - HALO + lax.broadcasted_iota(jnp.int32, (tm + 2 * HALO, 1), 0)
    return (t >= 0) & (t < n_rows)


def _adaln_kernel(cb_ref, w_ref, b_ref, o_ref):
    rows = []
    for r in range(2):
        cv = cb_ref[r]
        s = cv / (1.0 + jnp.exp(-cv))
        parts = []
        for j in range(ADA_TN // LANES):
            wj = w_ref[:, j * LANES:(j + 1) * LANES]
            parts.append(jnp.sum(s * wj, axis=0, keepdims=True))
        rows.append(jnp.concatenate(parts, axis=1) + b_ref[...])
    pad = jnp.zeros((SUBLANES_F32 - 2, ADA_TN), _F32)
    o_ref[...] = jnp.concatenate(rows + [pad], axis=0)


def _adaln(c, c_ctx, w_ada, b_ada):
    d = w_ada.shape[0]
    n_out = w_ada.shape[1]
    cvec = jnp.stack([c[0], c_ctx], axis=0)
    cb = jnp.broadcast_to(cvec[:, :, None], (2, d, LANES))
    return pl.pallas_call(
        _adaln_kernel,
        grid=(n_out // ADA_TN,),
        in_specs=[pl.BlockSpec((2, d, LANES), lambda j: (0, 0, 0)),
                  pl.BlockSpec((d, ADA_TN), lambda j: (0, j)),
                  pl.BlockSpec((1, ADA_TN), lambda j: (0, j))],
        out_specs=pl.BlockSpec((SUBLANES_F32, ADA_TN), lambda j: (0, j)),
        out_shape=jax.ShapeDtypeStruct((SUBLANES_F32, n_out), _F32),
        compiler_params=_params(1),
        name="adaln",
    )(cb, w_ada, b_ada.reshape(1, n_out))


def _rope_kernel(freq_ref, sign_ref, cos_ref, sin_ref):
    i = pl.program_id(0)
    t = i * ROPE_TR + lax.broadcasted_iota(jnp.int32, (ROPE_TR, LANES), 0)
    lane = lax.broadcasted_iota(jnp.int32, (ROPE_TR, LANES), 1)
    shift = int(math.log2(GRID_W))
    pos = jnp.where(lane < ROPE_AXIS_DIM, lax.shift_right_logical(t, shift), t & (GRID_W - 1))
    ang = pos.astype(_F32) * freq_ref[...]
    cos_ref[...] = jnp.cos(ang)
    sin_ref[...] = jnp.sin(ang) * sign_ref[...]


def _rope_tables(n):
    assert GRID_W & (GRID_W - 1) == 0
    half = ROPE_AXIS_DIM // 2
    freqs = ROPE_THETA ** (-jnp.arange(half, dtype=_F32) / half)
    freq = jnp.tile(freqs, HEAD_DIM // half).reshape(1, HEAD_DIM)
    sign = jnp.tile(jnp.concatenate([-jnp.ones((half,), _F32), jnp.ones((half,), _F32)]),
                    HEAD_DIM // ROPE_AXIS_DIM).reshape(1, HEAD_DIM)
    tr = min(ROPE_TR, n)
    assert n % tr == 0 and tr == ROPE_TR
    vec = pl.BlockSpec((1, HEAD_DIM), lambda i: (0, 0))
    tab = pl.BlockSpec((ROPE_TR, HEAD_DIM), lambda i: (i, 0))
    return pl.pallas_call(
        _rope_kernel,
        grid=(n // ROPE_TR,),
        in_specs=[vec, vec],
        out_specs=[tab, tab],
        out_shape=[jax.ShapeDtypeStruct((n, HEAD_DIM), _F32)] * 2,
        compiler_params=_params(1),
        name="rope_tables",
    )(freq, sign)


def _rope_apply(x, cos, sin_signed):
    half = ROPE_AXIS_DIM // 2
    lane = lax.broadcasted_iota(jnp.int32, x.shape, 1)
    first = (lane & (ROPE_AXIS_DIM - 1)) < half
    swapped = jnp.where(first, pltpu.roll(x, HEAD_DIM - half, axis=1), pltpu.roll(x, half, axis=1))
    return x * cos + swapped * sin_signed


def _ctx_kv_kernel(x_ref, n1_ref, sh_ref, sc_ref, w_ref, kn_ref, k_ref, v_ref):
    x = x_ref[...]
    h = _rms(x, n1_ref[...]) * (1.0 + sc_ref[1:2, :]) + sh_ref[1:2, :]
    p = jnp.dot(h.astype(_BF16), w_ref[...], preferred_element_type=_F32)
    kvw = N_KV_HEADS * HEAD_DIM
    for g in range(N_KV_HEADS):
        kg = p[:, g * HEAD_DIM:(g + 1) * HEAD_DIM]
        k_ref[:, g * HEAD_DIM:(g + 1) * HEAD_DIM] = _rms(kg, kn_ref[...]).astype(_BF16)
    v_ref[...] = p[:, kvw:].astype(_BF16)


def _ctx_kv(ctx2d, norm1_w, mod, w_in_b, k_norm_w):
    c_len, d = ctx2d.shape
    attn_w = N_HEADS * HEAD_DIM
    kvw = N_KV_HEADS * HEAD_DIM
    assert attn_w % (2 * kvw) == 0
    return pl.pallas_call(
        _ctx_kv_kernel,
        grid=(1,),
        in_specs=[pl.BlockSpec((c_len, d), lambda i: (0, 0)),
                  pl.BlockSpec((1, d), lambda i: (0, 0)),
                  pl.BlockSpec((SUBLANES_F32, d), lambda i: (0, 0)),
                  pl.BlockSpec((SUBLANES_F32, d), lambda i: (0, 1)),
                  pl.BlockSpec((d, 2 * kvw), lambda i: (0, attn_w // (2 * kvw))),
                  pl.BlockSpec((1, HEAD_DIM), lambda i: (0, 0))],
        out_specs=[pl.BlockSpec((c_len, kvw), lambda i: (0, 0))] * 2,
        out_shape=[jax.ShapeDtypeStruct((c_len, kvw), _BF16)] * 2,
        compiler_params=_params(1),
        name="ctx_kv",
    )(ctx2d, norm1_w, mod, mod, w_in_b, k_norm_w)


def _in_proj_kernel(n_rows, xp_ref, xm_ref, xn_ref, n1_ref, sh_ref, sc_ref, w_ref, qn_ref, kn_ref,
                    cos_ref, sin_ref, cw_ref, con_ref, q_ref, k_ref, v_ref, conv_ref):
    i = pl.program_id(0)
    tm = xm_ref.shape[0]
    attn_w = N_HEADS * HEAD_DIM
    kvw = N_KV_HEADS * HEAD_DIM
    conv_w = conv_ref.shape[1]

    xe = jnp.concatenate([xp_ref[...], xm_ref[...], xn_ref[...]], axis=0)
    h = _rms(xe, n1_ref[...]) * (1.0 + sc_ref[0:1, :]) + sh_ref[0:1, :]
    hb_ext = h.astype(_BF16)
    hb = hb_ext[HALO:HALO + tm]
    cos = cos_ref[...]
    sin = sin_ref[...]

    def proj(lhs, lo, width):
        return jnp.dot(lhs, w_ref[:, lo:lo + width], preferred_element_type=_F32)

    step = 2 * HEAD_DIM
    for c0 in range(0, attn_w, step):
        pq = proj(hb, c0, step)
        for s in range(0, step, HEAD_DIM):
            qh = _rms(pq[:, s:s + HEAD_DIM], qn_ref[...])
            q_ref[:, c0 + s:c0 + s + HEAD_DIM] = _rope_apply(qh, cos, sin).astype(_BF16)
    pk = proj(hb, attn_w, kvw)
    for s in range(0, kvw, HEAD_DIM):
        kh = _rms(pk[:, s:s + HEAD_DIM], kn_ref[...])
        k_ref[:, s:s + HEAD_DIM] = _rope_apply(kh, cos, sin).astype(_BF16)
    v_ref[...] = proj(hb, attn_w + kvw, kvw).astype(_BF16)

    o = attn_w + 2 * kvw
    valid = _halo_valid(i, tm, n_rows)
    cb = proj(hb, o, conv_w)
    u = proj(hb_ext, o + conv_w, conv_w) * proj(hb_ext, o + 2 * conv_w, conv_w)
    u = jnp.where(valid, u, 0.0)
    up, uc, un = _shift_rows(u, tm)
    y = cb * (up * cw_ref[0:1, :] + uc * cw_ref[1:2, :] + un * cw_ref[2:3, :])
    conv_ref[...] = _rms(y, con_ref[...]).astype(_BF16)


def _in_proj(x2d, norm1_w, mod, w_in_b, qn_scaled, k_norm_w, cos, sin, conv_w, con_w):
    n, d = x2d.shape
    tm = TM_IN
    assert n % tm == 0 and tm % HALO == 0
    attn_w = N_HEADS * HEAD_DIM
    kvw = N_KV_HEADS * HEAD_DIM
    cw = conv_w.shape[1]
    r = tm // HALO
    last_halo = n // HALO - 1
    vec = lambda width: pl.BlockSpec((1, width), lambda i: (0, 0))
    row = lambda width: pl.BlockSpec((tm, width), lambda i: (i, 0))
    return pl.pallas_call(
        functools.partial(_in_proj_kernel, n),
        grid=(n // tm,),
        in_specs=[pl.BlockSpec((HALO, d), lambda i: (jnp.maximum(i * r - 1, 0), 0)),
                  row(d),
                  pl.BlockSpec((HALO, d), lambda i: (jnp.minimum((i + 1) * r, last_halo), 0)),
                  vec(d),
                  pl.BlockSpec((SUBLANES_F32, d), lambda i: (0, 0)),
                  pl.BlockSpec((SUBLANES_F32, d), lambda i: (0, 1)),
                  pl.BlockSpec(w_in_b.shape, lambda i: (0, 0), pipeline_mode=pl.Buffered(1)),
                  vec(HEAD_DIM), vec(HEAD_DIM),
                  row(HEAD_DIM), row(HEAD_DIM),
                  pl.BlockSpec((CONV_K, cw), lambda i: (0, 0)),
                  vec(cw)],
        out_specs=[row(attn_w), row(kvw), row(kvw), row(cw)],
        out_shape=[jax.ShapeDtypeStruct((n, attn_w), _BF16),
                   jax.ShapeDtypeStruct((n, kvw), _BF16),
                   jax.ShapeDtypeStruct((n, kvw), _BF16),
                   jax.ShapeDtypeStruct((n, cw), _BF16)],
        compiler_params=_params(1),
        name="in_proj",
    )(x2d, x2d, x2d, norm1_w, mod, mod, w_in_b, qn_scaled, k_norm_w, cos, sin, conv_w, con_w)


def _attn_kernel(q_ref, kc_ref, vc_ref, k_ref, v_ref, o_ref, m_ref, acc_ref, sa_ref, sb_ref, sc_ref):
    tq = q_ref.shape[0]
    nk = k_ref.shape[0] // TK
    qg = jnp.concatenate([q_ref[:, j * HEAD_DIM:(j + 1) * HEAD_DIM] for j in range(GROUP)], axis=0)

    def scores(kb):
        return lax.dot_general(qg, kb, (((1,), (1,)), ((), ())), preferred_element_type=_F32)

    def keys(b):
        return k_ref[pl.ds(pl.multiple_of(b * TK, TK), TK), :]

    def values(b):
        return v_ref[pl.ds(pl.multiple_of(b * TK, TK), TK), :]

    def process(s_ref, vb):
        tk = vb.shape[0]
        sc = [s_ref[:, c * LANES:(c + 1) * LANES] for c in range(tk // LANES)]
        mc = functools.reduce(jnp.maximum, sc)
        m_prev = m_ref[...]
        m_new = jnp.maximum(m_prev, jnp.max(mc, axis=1, keepdims=True))
        alpha = jnp.exp2(m_prev - m_new)
        p = jnp.concatenate([jnp.exp2(x - m_new).astype(_BF16) for x in sc], axis=1)
        vb_ones = jnp.concatenate([vb, jnp.ones((tk, HEAD_DIM), _BF16)], axis=1)
        pv = jnp.dot(p, vb_ones, preferred_element_type=_F32)
        acc_ref[...] = jnp.concatenate([alpha, alpha], axis=1) * acc_ref[...] + pv
        m_ref[...] = m_new

    m_ref[...] = jnp.full(m_ref.shape, -jnp.inf, _F32)
    acc_ref[...] = jnp.zeros(acc_ref.shape, _F32)
    sc_ref[...] = scores(kc_ref[...])
    sa_ref[...] = scores(keys(0))
    process(sc_ref, vc_ref[...])

    def body(t, carry):
        b0 = 2 * t
        sb_ref[...] = scores(keys(b0 + 1))
        process(sa_ref, values(b0))
        sa_ref[...] = scores(keys(b0 + 2))
        process(sb_ref, values(b0 + 1))
        return carry

    lax.fori_loop(0, nk // 2 - 1, body, 0)
    sb_ref[...] = scores(keys(nk - 1))
    process(sa_ref, values(nk - 2))
    process(sb_ref, values(nk - 1))

    og = acc_ref[:, 0:HEAD_DIM] / acc_ref[:, HEAD_DIM:2 * HEAD_DIM]
    for j in range(GROUP):
        o_ref[:, j * HEAD_DIM:(j + 1) * HEAD_DIM] = og[j * tq:(j + 1) * tq].astype(_BF16)


def _attention(q, kc, vc, k, v):
    n, attn_w = q.shape
    c_len = kc.shape[0]
    gw = GROUP * HEAD_DIM
    assert n % TQ == 0 and n % (2 * TK) == 0 and c_len % LANES == 0
    rows = GROUP * TQ
    kv = lambda a: pl.BlockSpec((a.shape[0], HEAD_DIM), lambda g, i: (0, g), pipeline_mode=pl.Buffered(1))
    return pl.pallas_call(
        _attn_kernel,
        grid=(N_KV_HEADS, n // TQ),
        in_specs=[pl.BlockSpec((TQ, gw), lambda g, i: (i, g)),
                  kv(kc), kv(vc), kv(k), kv(v)],
        out_specs=pl.BlockSpec((TQ, gw), lambda g, i: (i, g)),
        out_shape=jax.ShapeDtypeStruct((n, attn_w), _BF16),
        scratch_shapes=[pltpu.VMEM((rows, LANES), _F32),
                        pltpu.VMEM((rows, 2 * HEAD_DIM), _F32),
                        pltpu.VMEM((rows, TK), _F32),
                        pltpu.VMEM((rows, TK), _F32),
                        pltpu.VMEM((rows, c_len), _F32)],
        compiler_params=_params(2),
        name="attention",
    )(q, kc, vc, k, v)


def _merge_kernel(a_ref, c_ref, aon_ref, w_ref, x_ref, g_ref, o_ref):
    aw = a_ref.shape[1]
    an = _rms(a_ref[...].astype(_F32), aon_ref[...]).astype(_BF16)
    y = jnp.dot(an, w_ref[0:aw, :], preferred_element_type=_F32)
    y = y + jnp.dot(c_ref[...], w_ref[aw:, :], preferred_element_type=_F32)
    o_ref[...] = x_ref[...] + g_ref[0:1, :] * y


def _merge(attn, conv_n, aon_w, w_o_b, x2d, mod):
    n, d = x2d.shape
    tm = TM_MERGE
    assert n % tm == 0
    row = lambda width: pl.BlockSpec((tm, width), lambda i: (i, 0))
    return pl.pallas_call(
        _merge_kernel,
        grid=(n // tm,),
        in_specs=[row(attn.shape[1]), row(conv_n.shape[1]),
                  pl.BlockSpec((1, attn.shape[1]), lambda i: (0, 0)),
                  pl.BlockSpec(w_o_b.shape, lambda i: (0, 0), pipeline_mode=pl.Buffered(1)),
                  row(d),
                  pl.BlockSpec((SUBLANES_F32, d), lambda i: (0, 2))],
        out_specs=row(d),
        out_shape=jax.ShapeDtypeStruct((n, d), _F32),
        compiler_params=_params(1),
        name="merge",
    )(attn, conv_n, aon_w, w_o_b, x2d, mod)


def _ffn_kernel(n_rows, xp_ref, xm_ref, xn_ref, n2_ref, sh_ref, sc_ref, g_ref, wa_ref, wg_ref,
                cwa_ref, cwg_ref, wd_ref, fn_ref, o_ref, h_ref, acc_ref):
    i = pl.program_id(0)
    j = pl.program_id(1)
    tm = xm_ref.shape[0]

    @pl.when(j == 0)
    def _():
        xe = jnp.concatenate([xp_ref[...], xm_ref[...], xn_ref[...]], axis=0)
        h = _rms(xe, n2_ref[...]) * (1.0 + sc_ref[0:1, :]) + sh_ref[0:1, :]
        h_ref[...] = jnp.where(_halo_valid(i, tm, n_rows), h, 0.0).astype(_BF16)
        acc_ref[...] = jnp.zeros_like(acc_ref)

    hb = h_ref[...]

    def conv(w_ref, cw_ref):
        u = jnp.dot(hb, w_ref[...], preferred_element_type=_F32)
        up, uc, un = _shift_rows(u, tm)
        return up * cw_ref[0:1, :] + uc * cw_ref[1:2, :] + un * cw_ref[2:3, :]

    a = conv(wa_ref, cwa_ref)
    g = conv(wg_ref, cwg_ref)
    act = (g / (1.0 + jnp.exp(-g))) * a
    acc_ref[...] += jnp.dot(act.astype(_BF16), wd_ref[...], preferred_element_type=_F32)

    @pl.when(j == pl.num_programs(1) - 1)
    def _():
        xs = xm_ref[...] + g_ref[0:1, :] * acc_ref[...]
        o_ref[...] = _rms(xs, fn_ref[...])


def _conv_ffn(xs, norm2_w, mod, w_up_b, ffn_conv_w, w_down_b, final_w):
    n, d = xs.shape
    d_ff = w_down_b.shape[0]
    tm, tf = TM_FFN, TF_FFN
    assert n % tm == 0 and d_ff % tf == 0 and tm % HALO == 0
    nf = d_ff // tf
    r = tm // HALO
    last_halo = n // HALO - 1
    vec = pl.BlockSpec((1, d), lambda i, j: (0, 0))
    modspec = lambda k: pl.BlockSpec((SUBLANES_F32, d), lambda i, j: (0, k))
    return pl.pallas_call(
        functools.partial(_ffn_kernel, n),
        grid=(n // tm, nf),
        in_specs=[pl.BlockSpec((HALO, d), lambda i, j: (jnp.maximum(i * r - 1, 0), 0)),
                  pl.BlockSpec((tm, d), lambda i, j: (i, 0)),
                  pl.BlockSpec((HALO, d), lambda i, j: (jnp.minimum((i + 1) * r, last_halo), 0)),
                  vec, modspec(3), modspec(4), modspec(5),
                  pl.BlockSpec((d, tf), lambda i, j: (0, j)),
                  pl.BlockSpec((d, tf), lambda i, j: (0, j + nf)),
                  pl.BlockSpec((CONV_K, tf), lambda i, j: (0, j)),
                  pl.BlockSpec((CONV_K, tf), lambda i, j: (0, j + nf)),
                  pl.BlockSpec((tf, d), lambda i, j: (j, 0)),
                  vec],
        out_specs=pl.BlockSpec((tm, d), lambda i, j: (i, 0)),
        out_shape=jax.ShapeDtypeStruct((n, d), _F32),
        scratch_shapes=[pltpu.VMEM((tm + 2 * HALO, d), _BF16),
                        pltpu.VMEM((tm, d), _F32)],
        compiler_params=_params(2),
        name="conv_ffn",
    )(xs, xs, xs, norm2_w, mod, mod, mod, w_up_b, w_up_b, ffn_conv_w, ffn_conv_w, w_down_b, final_w)


def kernel(x, c, ctx, c_ctx, w_ada, b_ada, norm1_w, w_in, q_norm_w, k_norm_w, conv_w, attn_out_norm_w,
           conv_out_norm_w, w_o, norm2_w, w_ffn_up, ffn_conv_w, w_ffn_down, final_norm_w):
    b, n, d = x.shape
    assert b == 1 and w_ada.shape[0] == 1, "single batch element, single trunk layer"
    x2d = x[0]
    row = lambda v: v.reshape(1, -1)

    mod = _adaln(c, c_ctx, w_ada[0], b_ada[0])
    cos, sin = _rope_tables(n)

    w_in_b = w_in[0].astype(_BF16)
    qn_scaled = row(q_norm_w[0]) * (HEAD_DIM ** -0.5 * math.log2(math.e))
    kc, vc = _ctx_kv(ctx[0], row(norm1_w[0]), mod, w_in_b, row(k_norm_w[0]))
    q, k, v, conv_n = _in_proj(x2d, row(norm1_w[0]), mod, w_in_b, qn_scaled, row(k_norm_w[0]),
                               cos, sin, conv_w[0], row(conv_out_norm_w[0]))
    attn = _attention(q, kc, vc, k, v)
    xs = _merge(attn, conv_n, row(attn_out_norm_w[0]), w_o[0].astype(_BF16), x2d, mod)
    out = _conv_ffn(xs, row(norm2_w[0]), mod, w_ffn_up[0].astype(_BF16), ffn_conv_w[0],
                    w_ffn_down[0].astype(_BF16), row(final_norm_w))
    return out[None]
```

```python
import functools
import math

import jax
import jax.numpy as jnp
from jax import lax
from jax.experimental import pallas as pl
from jax.experimental.pallas import tpu as pltpu

GRID_W = 64
N_HEADS = 8
N_KV_HEADS = 2
HEAD_DIM = 128
GROUP = N_HEADS // N_KV_HEADS
ROPE_THETA = 10000.0
ROPE_AXIS_DIM = HEAD_DIM // 2
EPS = 1e-6
N_MOD = 6
CONV_K = 3

LANES = 128
SUBLANES_F32 = 8
BF16_ROWS = 16
VMEM_LIMIT_BYTES = 60 * 1024 * 1024

ADA_TN = 1024
TM_IN = 512
TQ = 256
TK = 1024
ATTN_UNROLL = 2
TM_MERGE = 512
TM_FFN = 1024
TF_FFN = 512
HALO = SUBLANES_F32

_BF16 = jnp.bfloat16
_F32 = jnp.float32


def _params(n_axes):
    return pltpu.CompilerParams(dimension_semantics=("arbitrary",) * n_axes,
                                vmem_limit_bytes=VMEM_LIMIT_BYTES)


def _rms(x, w):
    return x * lax.rsqrt(jnp.mean(x * x, axis=-1, keepdims=True) + EPS) * w


def _shift_rows(u, tm):
    rows = u.shape[0]
    prev = pltpu.roll(u, 1, axis=0)[HALO:HALO + tm]
    nxt = pltpu.roll(u, rows - 1, axis=0)[HALO:HALO + tm]
    return prev, u[HALO:HALO + tm], nxt


def _halo_valid(i, tm, n_rows):
    t = i * tm - HALO + lax.broadcasted_iota(jnp.int32, (tm + 2 * HALO, 1), 0)
    return (t >= 0) & (t < n_rows)


def _adaln_kernel(cb_ref, w_ref, b_ref, o_ref):
    rows = []
    for r in range(2):
        cv = cb_ref[r]
        s = cv / (1.0 + jnp.exp(-cv))
        parts = []
        for j in range(ADA_TN // LANES):
            wj = w_ref[:, j * LANES:(j + 1) * LANES]
            parts.append(jnp.sum(s * wj, axis=0, keepdims=True))
        rows.append(jnp.concatenate(parts, axis=1) + b_ref[...])
    pad = jnp.zeros((SUBLANES_F32 - 2, ADA_TN), _F32)
    o_ref[...] = jnp.concatenate(rows + [pad], axis=0)


def _adaln(c, c_ctx, w_ada, b_ada):
    d = w_ada.shape[0]
    n_out = w_ada.shape[1]
    cvec = jnp.stack([c[0], c_ctx], axis=0)
    cb = jnp.broadcast_to(cvec[:, :, None], (2, d, LANES))
    return pl.pallas_call(
        _adaln_kernel,
        grid=(n_out // ADA_TN,),
        in_specs=[pl.BlockSpec((2, d, LANES), lambda j: (0, 0, 0)),
                  pl.BlockSpec((d, ADA_TN), lambda j: (0, j)),
                  pl.BlockSpec((1, ADA_TN), lambda j: (0, j))],
        out_specs=pl.BlockSpec((SUBLANES_F32, ADA_TN), lambda j: (0, j)),
        out_shape=jax.ShapeDtypeStruct((SUBLANES_F32, n_out), _F32),
        compiler_params=_params(1),
        name="adaln",
    )(cb, w_ada, b_ada.reshape(1, n_out))


def _rope_kernel(freq_ref, sign_ref, cos_ref, sin_ref):
    pos = lax.broadcasted_iota(jnp.int32, cos_ref.shape, 0)
    ang = pos.astype(_F32) * freq_ref[...]
    cos_ref[...] = jnp.cos(ang)
    sin_ref[...] = jnp.sin(ang) * sign_ref[...]


def _rope_tables(n_pos):
    half = ROPE_AXIS_DIM // 2
    freqs = ROPE_THETA ** (-jnp.arange(half, dtype=_F32) / half)
    freq = jnp.tile(freqs, HEAD_DIM // half).reshape(1, HEAD_DIM)
    sign = jnp.tile(jnp.concatenate([-jnp.ones((half,), _F32), jnp.ones((half,), _F32)]),
                    HEAD_DIM // ROPE_AXIS_DIM).reshape(1, HEAD_DIM)
    vec = pl.BlockSpec((1, HEAD_DIM), lambda i: (0, 0))
    tab = pl.BlockSpec((n_pos, HEAD_DIM), lambda i: (0, 0))
    return pl.pallas_call(
        _rope_kernel,
        grid=(1,),
        in_specs=[vec, vec],
        out_specs=[tab, tab],
        out_shape=[jax.ShapeDtypeStruct((n_pos, HEAD_DIM), _F32)] * 2,
        compiler_params=_params(1),
        name="rope_tables",
    )(freq, sign)


def _rope_tile(row_ref, col_ref):
    lane = lax.broadcasted_iota(jnp.int32, (GRID_W, HEAD_DIM), 1)
    col = col_ref[...]
    parts = [jnp.where(lane < ROPE_AXIS_DIM, jnp.broadcast_to(row_ref[g:g + 1, :], (GRID_W, HEAD_DIM)), col)
             for g in range(row_ref.shape[0])]
    return jnp.concatenate(parts, axis=0)


def _rope_apply(x, cos, sin_signed):
    half = ROPE_AXIS_DIM // 2
    lane = lax.broadcasted_iota(jnp.int32, x.shape, 1)
    first = (lane & (ROPE_AXIS_DIM - 1)) < half
    swapped = jnp.where(first, pltpu.roll(x, HEAD_DIM - half, axis=1), pltpu.roll(x, half, axis=1))
    return x * cos + swapped * sin_signed


def _ctx_kv_kernel(x_ref, n1_ref, sh_ref, sc_ref, w_ref, kn_ref, k_ref, v_ref):
    x = x_ref[...]
    h = _rms(x, n1_ref[...]) * (1.0 + sc_ref[1:2, :]) + sh_ref[1:2, :]
    p = jnp.dot(h.astype(_BF16), w_ref[...], preferred_element_type=_F32)
    kvw = N_KV_HEADS * HEAD_DIM
    for g in range(N_KV_HEADS):
        kg = p[:, g * HEAD_DIM:(g + 1) * HEAD_DIM]
        k_ref[:, g * HEAD_DIM:(g + 1) * HEAD_DIM] = _rms(kg, kn_ref[...]).astype(_BF16)
    v_ref[...] = p[:, kvw:].astype(_BF16)


def _ctx_kv(ctx2d, norm1_w, mod, w_in_b, k_norm_w):
    c_len, d = ctx2d.shape
    attn_w = N_HEADS * HEAD_DIM
    kvw = N_KV_HEADS * HEAD_DIM
    assert attn_w % (2 * kvw) == 0
    return pl.pallas_call(
        _ctx_kv_kernel,
        grid=(1,),
        in_specs=[pl.BlockSpec((c_len, d), lambda i: (0, 0)),
                  pl.BlockSpec((1, d), lambda i: (0, 0)),
                  pl.BlockSpec((SUBLANES_F32, d), lambda i: (0, 0)),
                  pl.BlockSpec((SUBLANES_F32, d), lambda i: (0, 1)),
                  pl.BlockSpec((d, 2 * kvw), lambda i: (0, attn_w // (2 * kvw))),
                  pl.BlockSpec((1, HEAD_DIM), lambda i: (0, 0))],
        out_specs=[pl.BlockSpec((c_len, kvw), lambda i: (0, 0))] * 2,
        out_shape=[jax.ShapeDtypeStruct((c_len, kvw), _BF16)] * 2,
        compiler_params=_params(1),
        name="ctx_kv",
    )(ctx2d, norm1_w, mod, mod, w_in_b, k_norm_w)


def _in_proj_kernel(n_rows, xp_ref, xm_ref, xn_ref, n1_ref, sh_ref, sc_ref, w_ref, qn_ref, kn_ref,
                    cos_row_ref, cos_col_ref, sin_row_ref, sin_col_ref, cw_ref, con_ref,
                    q_ref, k_ref, v_ref, conv_ref):
    i = pl.program_id(0)
    tm = xm_ref.shape[0]
    attn_w = N_HEADS * HEAD_DIM
    kvw = N_KV_HEADS * HEAD_DIM
    conv_w = conv_ref.shape[1]

    xe = jnp.concatenate([xp_ref[...], xm_ref[...], xn_ref[...]], axis=0)
    h = _rms(xe, n1_ref[...]) * (1.0 + sc_ref[0:1, :]) + sh_ref[0:1, :]
    hb_ext = h.astype(_BF16)
    hb = hb_ext[HALO:HALO + tm]
    cos = _rope_tile(cos_row_ref, cos_col_ref)
    sin = _rope_tile(sin_row_ref, sin_col_ref)

    def proj(lhs, lo, width):
        return jnp.dot(lhs, w_ref[:, lo:lo + width], preferred_element_type=_F32)

    step = 2 * HEAD_DIM
    for c0 in range(0, attn_w, step):
        pq = proj(hb, c0, step)
        for s in range(0, step, HEAD_DIM):
            qh = _rms(pq[:, s:s + HEAD_DIM], qn_ref[...])
            q_ref[:, c0 + s:c0 + s + HEAD_DIM] = _rope_apply(qh, cos, sin).astype(_BF16)
    pk = proj(hb, attn_w, kvw)
    for s in range(0, kvw, HEAD_DIM):
        kh = _rms(pk[:, s:s + HEAD_DIM], kn_ref[...])
        k_ref[:, s:s + HEAD_DIM] = _rope_apply(kh, cos, sin).astype(_BF16)
    v_ref[...] = proj(hb, attn_w + kvw, kvw).astype(_BF16)

    o = attn_w + 2 * kvw
    valid = _halo_valid(i, tm, n_rows)
    cb = proj(hb, o, conv_w)
    u = proj(hb_ext, o + conv_w, conv_w) * proj(hb_ext, o + 2 * conv_w, conv_w)
    u = jnp.where(valid, u, 0.0)
    up, uc, un = _shift_rows(u, tm)
    y = cb * (up * cw_ref[0:1, :] + uc * cw_ref[1:2, :] + un * cw_ref[2:3, :])
    conv_ref[...] = _rms(y, con_ref[...]).astype(_BF16)


def _in_proj(x2d, norm1_w, mod, w_in_b, qn_scaled, k_norm_w, cos, sin, conv_w, con_w):
    n, d = x2d.shape
    tm = TM_IN
    assert n % tm == 0 and tm % HALO == 0
    assert tm == SUBLANES_F32 * GRID_W and cos.shape[0] >= max(n // GRID_W, GRID_W)
    rope_row = pl.BlockSpec((SUBLANES_F32, HEAD_DIM), lambda i: (i, 0))
    rope_col = pl.BlockSpec((GRID_W, HEAD_DIM), lambda i: (0, 0))
    attn_w = N_HEADS * HEAD_DIM
    kvw = N_KV_HEADS * HEAD_DIM
    cw = conv_w.shape[1]
    r = tm // HALO
    last_halo = n // HALO - 1
    vec = lambda width: pl.BlockSpec((1, width), lambda i: (0, 0))
    row = lambda width: pl.BlockSpec((tm, width), lambda i: (i, 0))
    return pl.pallas_call(
        functools.partial(_in_proj_kernel, n),
        grid=(n // tm,),
        in_specs=[pl.BlockSpec((HALO, d), lambda i: (jnp.maximum(i * r - 1, 0), 0)),
                  row(d),
                  pl.BlockSpec((HALO, d), lambda i: (jnp.minimum((i + 1) * r, last_halo), 0)),
                  vec(d),
                  pl.BlockSpec((SUBLANES_F32, d), lambda i: (0, 0)),
                  pl.BlockSpec((SUBLANES_F32, d), lambda i: (0, 1)),
                  pl.BlockSpec(w_in_b.shape, lambda i: (0, 0), pipeline_mode=pl.Buffered(1)),
                  vec(HEAD_DIM), vec(HEAD_DIM),
                  rope_row, rope_col, rope_row, rope_col,
                  pl.BlockSpec((CONV_K, cw), lambda i: (0, 0)),
                  vec(cw)],
        out_specs=[row(attn_w), row(kvw), row(kvw), row(cw)],
        out_shape=[jax.ShapeDtypeStruct((n, attn_w), _BF16),
                   jax.ShapeDtypeStruct((n, kvw), _BF16),
                   jax.ShapeDtypeStruct((n, kvw), _BF16),
                   jax.ShapeDtypeStruct((n, cw), _BF16)],
        compiler_params=_params(1),
        name="in_proj",
    )(x2d, x2d, x2d, norm1_w, mod, mod, w_in_b, qn_scaled, k_norm_w, cos, cos, sin, sin, conv_w, con_w)


def _attn_kernel(q_ref, kc_ref, vc_ref, k_ref, v_ref, w1_ref, w2_ref, w3_ref,
                 o_ref, w1b_ref, w2b_ref, w3b_ref, m_ref, acc_ref, sa_ref, sb_ref, sc_ref):
    for src, dst in ((w1_ref, w1b_ref), (w2_ref, w2b_ref), (w3_ref, w3b_ref)):
        dst[...] = src[...].astype(_BF16)
    tq = q_ref.shape[0]
    nk = k_ref.shape[0] // TK
    qg = jnp.concatenate([q_ref[:, j * HEAD_DIM:(j + 1) * HEAD_DIM] for j in range(GROUP)], axis=0)

    def scores(kb):
        return lax.dot_general(qg, kb, (((1,), (1,)), ((), ())), preferred_element_type=_F32)

    def keys(b):
        return k_ref[pl.ds(pl.multiple_of(b * TK, TK), TK), :]

    def values(b):
        return v_ref[pl.ds(pl.multiple_of(b * TK, TK), TK), :]

    def process(s_ref, vb):
        tk = vb.shape[0]
        sc = [s_ref[:, c * LANES:(c + 1) * LANES] for c in range(tk // LANES)]
        mc = functools.reduce(jnp.maximum, sc)
        m_prev = m_ref[...]
        m_new = jnp.maximum(m_prev, jnp.max(mc, axis=1, keepdims=True))
        alpha = jnp.exp2(m_prev - m_new)
        p = jnp.concatenate([jnp.exp2(x - m_new).astype(_BF16) for x in sc], axis=1)
        vb_ones = jnp.concatenate([vb, jnp.ones((tk, HEAD_DIM), _BF16)], axis=1)
        pv = jnp.dot(p, vb_ones, preferred_element_type=_F32)
        acc_ref[...] = jnp.concatenate([alpha, alpha], axis=1) * acc_ref[...] + pv
        m_ref[...] = m_new

    m_ref[...] = jnp.full(m_ref.shape, -jnp.inf, _F32)
    acc_ref[...] = jnp.zeros(acc_ref.shape, _F32)
    sc_ref[...] = scores(kc_ref[...])
    sa_ref[...] = scores(keys(0))
    process(sc_ref, vc_ref[...])

    def body(t, carry):
        b0 = 2 * t
        sb_ref[...] = scores(keys(b0 + 1))
        process(sa_ref, values(b0))
        sa_ref[...] = scores(keys(b0 + 2))
        process(sb_ref, values(b0 + 1))
        return carry

    lax.fori_loop(0, nk // 2 - 1, body, 0, unroll=ATTN_UNROLL)
    sb_ref[...] = scores(keys(nk - 1))
    process(sa_ref, values(nk - 2))
    process(sb_ref, values(nk - 1))

    og = acc_ref[:, 0:HEAD_DIM] / acc_ref[:, HEAD_DIM:2 * HEAD_DIM]
    for j in range(GROUP):
        o_ref[:, j * HEAD_DIM:(j + 1) * HEAD_DIM] = og[j * tq:(j + 1) * tq].astype(_BF16)


def _attention(q, kc, vc, k, v, f32_weights):
    n, attn_w = q.shape
    c_len = kc.shape[0]
    gw = GROUP * HEAD_DIM
    assert n % TQ == 0 and n % (2 * TK) == 0 and c_len % LANES == 0
    rows = GROUP * TQ
    nq = n // TQ
    steps = N_KV_HEADS * nq
    kv = lambda a: pl.BlockSpec((a.shape[0], HEAD_DIM), lambda g, i: (0, g), pipeline_mode=pl.Buffered(1))

    def slab(w):
        assert w.shape[0] % (steps * BF16_ROWS) == 0, w.shape
        return pl.BlockSpec((w.shape[0] // steps, w.shape[1]), lambda g, i: (g * nq + i, 0))

    slabs = [slab(w) for w in f32_weights]
    outs = pl.pallas_call(
        _attn_kernel,
        grid=(N_KV_HEADS, nq),
        in_specs=[pl.BlockSpec((TQ, gw), lambda g, i: (i, g)),
                  kv(kc), kv(vc), kv(k), kv(v)] + slabs,
        out_specs=[pl.BlockSpec((TQ, gw), lambda g, i: (i, g))] + slabs,
        out_shape=[jax.ShapeDtypeStruct((n, attn_w), _BF16)]
        + [jax.ShapeDtypeStruct(w.shape, _BF16) for w in f32_weights],
        scratch_shapes=[pltpu.VMEM((rows, LANES), _F32),
                        pltpu.VMEM((rows, 2 * HEAD_DIM), _F32),
                        pltpu.VMEM((rows, TK), _F32),
                        pltpu.VMEM((rows, TK), _F32),
                        pltpu.VMEM((rows, c_len), _F32)],
        compiler_params=_params(2),
        name="attention",
    )(q, kc, vc, k, v, *f32_weights)
    return outs[0], outs[1:]


def _merge_kernel(a_ref, c_ref, aon_ref, w_ref, x_ref, g_ref, o_ref):
    aw = a_ref.shape[1]
    an = _rms(a_ref[...].astype(_F32), aon_ref[...]).astype(_BF16)
    y = jnp.dot(an, w_ref[0:aw, :], preferred_element_type=_F32)
    y = y + jnp.dot(c_ref[...], w_ref[aw:, :], preferred_element_type=_F32)
    o_ref[...] = x_ref[...] + g_ref[0:1, :] * y


def _merge(attn, conv_n, aon_w, w_o_b, x2d, mod):
    n, d = x2d.shape
    tm = TM_MERGE
    assert n % tm == 0
    row = lambda width: pl.BlockSpec((tm, width), lambda i: (i, 0))
    return pl.pallas_call(
        _merge_kernel,
        grid=(n // tm,),
        in_specs=[row(attn.shape[1]), row(conv_n.shape[1]),
                  pl.BlockSpec((1, attn.shape[1]), lambda i: (0, 0)),
                  pl.BlockSpec(w_o_b.shape, lambda i: (0, 0), pipeline_mode=pl.Buffered(1)),
                  row(d),
                  pl.BlockSpec((SUBLANES_F32, d), lambda i: (0, 2))],
        out_specs=row(d),
        out_shape=jax.ShapeDtypeStruct((n, d), _F32),
        compiler_params=_params(1),
        name="merge",
    )(attn, conv_n, aon_w, w_o_b, x2d, mod)


def _ffn_kernel(n_rows, xp_ref, xm_ref, xn_ref, n2_ref, sh_ref, sc_ref, g_ref, wa_ref, wg_ref,
                cwa_ref, cwg_ref, wd_ref, fn_ref, o_ref, h_ref):
    i = pl.program_id(0)
    j = pl.program_id(1)
    tm = xm_ref.shape[0]

    @pl.when(j == 0)
    def _():
        xe = jnp.concatenate([xp_ref[...], xm_ref[...], xn_ref[...]], axis=0)
        h = _rms(xe, n2_ref[...]) * (1.0 + sc_ref[0:1, :]) + sh_ref[0:1, :]
        h_ref[...] = jnp.where(_halo_valid(i, tm, n_rows), h, 0.0).astype(_BF16)
        o_ref[...] = jnp.zeros_like(o_ref)

    hb = h_ref[...]

    def conv(w_ref, cw_ref):
        u = jnp.dot(hb, w_ref[...], preferred_element_type=_F32)
        up, uc, un = _shift_rows(u, tm)
        return up * cw_ref[0:1, :] + uc * cw_ref[1:2, :] + un * cw_ref[2:3, :]

    a = conv(wa_ref, cwa_ref)
    g = conv(wg_ref, cwg_ref)
    act = (g / (1.0 + jnp.exp(-g))) * a
    o_ref[...] += jnp.dot(act.astype(_BF16), wd_ref[...], preferred_element_type=_F32)

    @pl.when(j == pl.num_programs(1) - 1)
    def _():
        xs = xm_ref[...] + g_ref[0:1, :] * o_ref[...]
        o_ref[...] = _rms(xs, fn_ref[...])


def _conv_ffn(xs, norm2_w, mod, w_up_b, ffn_conv_w, w_down_b, final_w):
    n, d = xs.shape
    d_ff = w_down_b.shape[0]
    tm, tf = TM_FFN, TF_FFN
    assert n % tm == 0 and d_ff % tf == 0 and tm % HALO == 0
    nf = d_ff // tf
    r = tm // HALO
    last_halo = n // HALO - 1
    vec = pl.BlockSpec((1, d), lambda i, j: (0, 0))
    modspec = lambda k: pl.BlockSpec((SUBLANES_F32, d), lambda i, j: (0, k))
    return pl.pallas_call(
        functools.partial(_ffn_kernel, n),
        grid=(n // tm, nf),
        in_specs=[pl.BlockSpec((HALO, d), lambda i, j: (jnp.maximum(i * r - 1, 0), 0)),
                  pl.BlockSpec((tm, d), lambda i, j: (i, 0), pipeline_mode=pl.Buffered(1)),
                  pl.BlockSpec((HALO, d), lambda i, j: (jnp.minimum((i + 1) * r, last_halo), 0)),
                  vec, modspec(3), modspec(4), modspec(5),
                  pl.BlockSpec((d, tf), lambda i, j: (0, j)),
                  pl.BlockSpec((d, tf), lambda i, j: (0, j + nf)),
                  pl.BlockSpec((CONV_K, tf), lambda i, j: (0, j)),
                  pl.BlockSpec((CONV_K, tf), lambda i, j: (0, j + nf)),
                  pl.BlockSpec((tf, d), lambda i, j: (j, 0)),
                  vec],
        out_specs=pl.BlockSpec((tm, d), lambda i, j: (i, 0)),
        out_shape=jax.ShapeDtypeStruct((n, d), _F32),
        scratch_shapes=[pltpu.VMEM((tm + 2 * HALO, d), _BF16)],
        compiler_params=_params(2),
        name="conv_ffn",
    )(xs, xs, xs, norm2_w, mod, mod, mod, w_up_b, w_up_b, ffn_conv_w, ffn_conv_w, w_down_b, final_w)


def kernel(x, c, ctx, c_ctx, w_ada, b_ada, norm1_w, w_in, q_norm_w, k_norm_w, conv_w, attn_out_norm_w,
           conv_out_norm_w, w_o, norm2_w, w_ffn_up, ffn_conv_w, w_ffn_down, final_norm_w):
    b, n, d = x.shape
    assert b == 1 and w_ada.shape[0] == 1, "single batch element, single trunk layer"
    x2d = x[0]
    row = lambda v: v.reshape(1, -1)

    mod = _adaln(c, c_ctx, w_ada[0], b_ada[0])
    cos, sin = _rope_tables(max(n // GRID_W, GRID_W))

    w_in_b = w_in[0].astype(_BF16)
    qn_scaled = row(q_norm_w[0]) * (HEAD_DIM ** -0.5 * math.log2(math.e))
    kc, vc = _ctx_kv(ctx[0], row(norm1_w[0]), mod, w_in_b, row(k_norm_w[0]))
    q, k, v, conv_n = _in_proj(x2d, row(norm1_w[0]), mod, w_in_b, qn_scaled, row(k_norm_w[0]),
                               cos, sin, conv_w[0], row(conv_out_norm_w[0]))
    d_ff = w_ffn_down.shape[1]
    attn, (w_o_b, w_up_b, w_down_b) = _attention(
        q, kc, vc, k, v, [w_o[0], w_ffn_up[0], w_ffn_down[0].reshape(d, d_ff)])
    xs = _merge(attn, conv_n, row(attn_out_norm_w[0]), w_o_b, x2d, mod)
    out = _conv_ffn(xs, row(norm2_w[0]), mod, w_up_b, ffn_conv_w[0], w_down_b.reshape(d_ff, d),
                    row(final_norm_w))
    return out[None]
```

```python
import functools
import math

import jax
import jax.numpy as jnp
from jax import lax
from jax.experimental import pallas as pl
from jax.experimental.pallas import tpu as pltpu

GRID_W = 64
N_HEADS = 8
N_KV_HEADS = 2
HEAD_DIM = 128
GROUP = N_HEADS // N_KV_HEADS
ROPE_THETA = 10000.0
ROPE_AXIS_DIM = HEAD_DIM // 2
EPS = 1e-6
N_MOD = 6
CONV_K = 3

LANES = 128
SUBLANES_F32 = 8
BF16_ROWS = 16
VMEM_LIMIT_BYTES = 60 * 1024 * 1024

ADA_TN = 1024
TM_IN = 512
TQ = 256
TK = 1024
ATTN_UNROLL = 2
TM_MERGE = 512
TM_FFN = 1024
TF_FFN = 512
HALO = SUBLANES_F32

_BF16 = jnp.bfloat16
_F32 = jnp.float32


def _params(n_axes):
    return pltpu.CompilerParams(dimension_semantics=("arbitrary",) * n_axes,
                                vmem_limit_bytes=VMEM_LIMIT_BYTES)


def _rms(x, w):
    return x * lax.rsqrt(jnp.mean(x * x, axis=-1, keepdims=True) + EPS) * w


def _shift_rows(u, tm):
    rows = u.shape[0]
    prev = pltpu.roll(u, 1, axis=0)[HALO:HALO + tm]
    nxt = pltpu.roll(u, rows - 1, axis=0)[HALO:HALO + tm]
    return prev, u[HALO:HALO + tm], nxt


def _halo_valid(i, tm, n_rows):
    t = i * tm - HALO + lax.broadcasted_iota(jnp.int32, (tm + 2 * HALO, 1), 0)
    return (t >= 0) & (t < n_rows)


def _adaln_kernel(cb_ref, w_ref, b_ref, o_ref):
    rows = []
    for r in range(2):
        cv = cb_ref[r]
        s = cv / (1.0 + jnp.exp(-cv))
        parts = []
        for j in range(ADA_TN // LANES):
            wj = w_ref[:, j * LANES:(j + 1) * LANES]
            parts.append(jnp.sum(s * wj, axis=0, keepdims=True))
        rows.append(jnp.concatenate(parts, axis=1) + b_ref[...])
    pad = jnp.zeros((SUBLANES_F32 - 2, ADA_TN), _F32)
    o_ref[...] = jnp.concatenate(rows + [pad], axis=0)


def _adaln(c, c_ctx, w_ada, b_ada):
    d = w_ada.shape[0]
    n_out = w_ada.shape[1]
    cvec = jnp.stack([c[0], c_ctx], axis=0)
    cb = jnp.broadcast_to(cvec[:, :, None], (2, d, LANES))
    return pl.pallas_call(
        _adaln_kernel,
        grid=(n_out // ADA_TN,),
        in_specs=[pl.BlockSpec((2, d, LANES), lambda j: (0, 0, 0)),
                  pl.BlockSpec((d, ADA_TN), lambda j: (0, j)),
                  pl.BlockSpec((1, ADA_TN), lambda j: (0, j))],
        out_specs=pl.BlockSpec((SUBLANES_F32, ADA_TN), lambda j: (0, j)),
        out_shape=jax.ShapeDtypeStruct((SUBLANES_F32, n_out), _F32),
        compiler_params=_params(1),
        name="adaln",
    )(cb, w_ada, b_ada.reshape(1, n_out))


def _rope_kernel(freq_ref, sign_ref, cos_ref, sin_ref):
    pos = lax.broadcasted_iota(jnp.int32, cos_ref.shape, 0)
    ang = pos.astype(_F32) * freq_ref[...]
    cos_ref[...] = jnp.cos(ang)
    sin_ref[...] = jnp.sin(ang) * sign_ref[...]


def _rope_tables(n_pos):
    half = ROPE_AXIS_DIM // 2
    freqs = ROPE_THETA ** (-jnp.arange(half, dtype=_F32) / half)
    freq = jnp.tile(freqs, HEAD_DIM // half).reshape(1, HEAD_DIM)
    sign = jnp.tile(jnp.concatenate([-jnp.ones((half,), _F32), jnp.ones((half,), _F32)]),
                    HEAD_DIM // ROPE_AXIS_DIM).reshape(1, HEAD_DIM)
    vec = pl.BlockSpec((1, HEAD_DIM), lambda i: (0, 0))
    tab = pl.BlockSpec((n_pos, HEAD_DIM), lambda i: (0, 0))
    return pl.pallas_call(
        _rope_kernel,
        grid=(1,),
        in_specs=[vec, vec],
        out_specs=[tab, tab],
        out_shape=[jax.ShapeDtypeStruct((n_pos, HEAD_DIM), _F32)] * 2,
        compiler_params=_params(1),
        name="rope_tables",
    )(freq, sign)


def _rope_tile(row_ref, col_ref):
    lane = lax.broadcasted_iota(jnp.int32, (GRID_W, HEAD_DIM), 1)
    col = col_ref[...]
    parts = [jnp.where(lane < ROPE_AXIS_DIM, jnp.broadcast_to(row_ref[g:g + 1, :], (GRID_W, HEAD_DIM)), col)
             for g in range(row_ref.shape[0])]
    return jnp.concatenate(parts, axis=0)


def _rope_apply(x, cos, sin_signed):
    half = ROPE_AXIS_DIM // 2
    lane = lax.broadcasted_iota(jnp.int32, x.shape, 1)
    first = (lane & (ROPE_AXIS_DIM - 1)) < half
    swapped = jnp.where(first, pltpu.roll(x, HEAD_DIM - half, axis=1), pltpu.roll(x, half, axis=1))
    return x * cos + swapped * sin_signed


def _ctx_kv_kernel(x_ref, n1_ref, sh_ref, sc_ref, w_ref, kn_ref, k_ref, v_ref):
    x = x_ref[...]
    h = _rms(x, n1_ref[...]) * (1.0 + sc_ref[1:2, :]) + sh_ref[1:2, :]
    p = jnp.dot(h.astype(_BF16), w_ref[...], preferred_element_type=_F32)
    kvw = N_KV_HEADS * HEAD_DIM
    for g in range(N_KV_HEADS):
        kg = p[:, g * HEAD_DIM:(g + 1) * HEAD_DIM]
        k_ref[:, g * HEAD_DIM:(g + 1) * HEAD_DIM] = _rms(kg, kn_ref[...]).astype(_BF16)
    v_ref[...] = p[:, kvw:].astype(_BF16)


def _ctx_kv(ctx2d, norm1_w, mod, w_in_b, k_norm_w):
    c_len, d = ctx2d.shape
    attn_w = N_HEADS * HEAD_DIM
    kvw = N_KV_HEADS * HEAD_DIM
    assert attn_w % (2 * kvw) == 0
    return pl.pallas_call(
        _ctx_kv_kernel,
        grid=(1,),
        in_specs=[pl.BlockSpec((c_len, d), lambda i: (0, 0)),
                  pl.BlockSpec((1, d), lambda i: (0, 0)),
                  pl.BlockSpec((SUBLANES_F32, d), lambda i: (0, 0)),
                  pl.BlockSpec((SUBLANES_F32, d), lambda i: (0, 1)),
                  pl.BlockSpec((d, 2 * kvw), lambda i: (0, attn_w // (2 * kvw))),
                  pl.BlockSpec((1, HEAD_DIM), lambda i: (0, 0))],
        out_specs=[pl.BlockSpec((c_len, kvw), lambda i: (0, 0))] * 2,
        out_shape=[jax.ShapeDtypeStruct((c_len, kvw), _BF16)] * 2,
        compiler_params=_params(1),
        name="ctx_kv",
    )(ctx2d, norm1_w, mod, mod, w_in_b, k_norm_w)


def _in_proj_kernel(n_rows, xp_ref, xm_ref, xn_ref, n1_ref, sh_ref, sc_ref, w_ref, qn_ref, kn_ref,
                    cos_row_ref, cos_col_ref, sin_row_ref, sin_col_ref, cw_ref, con_ref,
                    q_ref, k_ref, v_ref, conv_ref):
    i = pl.program_id(0)
    tm = xm_ref.shape[0]
    attn_w = N_HEADS * HEAD_DIM
    kvw = N_KV_HEADS * HEAD_DIM
    conv_w = conv_ref.shape[1]

    xe = jnp.concatenate([xp_ref[...], xm_ref[...], xn_ref[...]], axis=0)
    h = _rms(xe, n1_ref[...]) * (1.0 + sc_ref[0:1, :]) + sh_ref[0:1, :]
    hb_ext = h.astype(_BF16)
    hb = hb_ext[HALO:HALO + tm]
    cos = _rope_tile(cos_row_ref, cos_col_ref)
    sin = _rope_tile(sin_row_ref, sin_col_ref)

    def proj(lhs, lo, width):
        return jnp.dot(lhs, w_ref[:, lo:lo + width], preferred_element_type=_F32)

    step = 2 * HEAD_DIM
    for c0 in range(0, attn_w, step):
        pq = proj(hb, c0, step)
        for s in range(0, step, HEAD_DIM):
            qh = _rms(pq[:, s:s + HEAD_DIM], qn_ref[...])
            q_ref[:, c0 + s:c0 + s + HEAD_DIM] = _rope_apply(qh, cos, sin).astype(_BF16)
    pk = proj(hb, attn_w, kvw)
    for s in range(0, kvw, HEAD_DIM):
        kh = _rms(pk[:, s:s + HEAD_DIM], kn_ref[...])
        k_ref[:, s:s + HEAD_DIM] = _rope_apply(kh, cos, sin).astype(_BF16)
    v_ref[...] = proj(hb, attn_w + kvw, kvw).astype(_BF16)

    o = attn_w + 2 * kvw
    valid = _halo_valid(i, tm, n_rows)
    cb = proj(hb, o, conv_w)
    u = proj(hb_ext, o + conv_w, conv_w) * proj(hb_ext, o + 2 * conv_w, conv_w)
    u = jnp.where(valid, u, 0.0)
    up, uc, un = _shift_rows(u, tm)
    y = cb * (up * cw_ref[0:1, :] + uc * cw_ref[1:2, :] + un * cw_ref[2:3, :])
    conv_ref[...] = _rms(y, con_ref[...]).astype(_BF16)


def _in_proj(x2d, norm1_w, mod, w_in_b, qn_scaled, k_norm_w, cos, sin, conv_w, con_w):
    n, d = x2d.shape
    tm = TM_IN
    assert n % tm == 0 and tm % HALO == 0
    assert tm == SUBLANES_F32 * GRID_W and cos.shape[0] >= max(n // GRID_W, GRID_W)
    rope_row = pl.BlockSpec((SUBLANES_F32, HEAD_DIM), lambda i: (i, 0))
    rope_col = pl.BlockSpec((GRID_W, HEAD_DIM), lambda i: (0, 0))
    attn_w = N_HEADS * HEAD_DIM
    kvw = N_KV_HEADS * HEAD_DIM
    cw = conv_w.shape[1]
    r = tm // HALO
    last_halo = n // HALO - 1
    vec = lambda width: pl.BlockSpec((1, width), lambda i: (0, 0))
    row = lambda width: pl.BlockSpec((tm, width), lambda i: (i, 0))
    return pl.pallas_call(
        functools.partial(_in_proj_kernel, n),
        grid=(n // tm,),
        in_specs=[pl.BlockSpec((HALO, d), lambda i: (jnp.maximum(i * r - 1, 0), 0)),
                  row(d),
                  pl.BlockSpec((HALO, d), lambda i: (jnp.minimum((i + 1) * r, last_halo), 0)),
                  vec(d),
                  pl.BlockSpec((SUBLANES_F32, d), lambda i: (0, 0)),
                  pl.BlockSpec((SUBLANES_F32, d), lambda i: (0, 1)),
                  pl.BlockSpec(w_in_b.shape, lambda i: (0, 0), pipeline_mode=pl.Buffered(1)),
                  vec(HEAD_DIM), vec(HEAD_DIM),
                  rope_row, rope_col, rope_row, rope_col,
                  pl.BlockSpec((CONV_K, cw), lambda i: (0, 0)),
                  vec(cw)],
        out_specs=[row(attn_w), row(kvw), row(kvw), row(cw)],
        out_shape=[jax.ShapeDtypeStruct((n, attn_w), _BF16),
                   jax.ShapeDtypeStruct((n, kvw), _BF16),
                   jax.ShapeDtypeStruct((n, kvw), _BF16),
                   jax.ShapeDtypeStruct((n, cw), _BF16)],
        compiler_params=_params(1),
        name="in_proj",
    )(x2d, x2d, x2d, norm1_w, mod, mod, w_in_b, qn_scaled, k_norm_w, cos, cos, sin, sin, conv_w, con_w)


def _attn_kernel(q_ref, kc_ref, vc_ref, k_ref, v_ref, w1_ref, w2_ref, w3_ref,
                 o_ref, w1b_ref, w2b_ref, w3b_ref, m_ref, acc_ref, sa_ref, sb_ref, sc_ref):
    for src, dst in ((w1_ref, w1b_ref), (w2_ref, w2b_ref), (w3_ref, w3b_ref)):
        if len(dst.shape) == 3:
            width = dst.shape[2]
            for c in range(dst.shape[0]):
                dst[c] = src[:, c * width:(c + 1) * width].astype(_BF16)
        else:
            dst[...] = src[...].astype(_BF16)
    tq = q_ref.shape[0]
    nk = k_ref.shape[0] // TK
    qg = jnp.concatenate([q_ref[:, j * HEAD_DIM:(j + 1) * HEAD_DIM] for j in range(GROUP)], axis=0)

    def scores(kb):
        return lax.dot_general(qg, kb, (((1,), (1,)), ((), ())), preferred_element_type=_F32)

    def keys(b):
        return k_ref[pl.ds(pl.multiple_of(b * TK, TK), TK), :]

    def values(b):
        return v_ref[pl.ds(pl.multiple_of(b * TK, TK), TK), :]

    def process(s_ref, vb):
        tk = vb.shape[0]
        sc = [s_ref[:, c * LANES:(c + 1) * LANES] for c in range(tk // LANES)]
        mc = functools.reduce(jnp.maximum, sc)
        m_prev = m_ref[...]
        m_new = jnp.maximum(m_prev, jnp.max(mc, axis=1, keepdims=True))
        alpha = jnp.exp2(m_prev - m_new)
        p = jnp.concatenate([jnp.exp2(x - m_new).astype(_BF16) for x in sc], axis=1)
        vb_ones = jnp.concatenate([vb, jnp.ones((tk, HEAD_DIM), _BF16)], axis=1)
        pv = jnp.dot(p, vb_ones, preferred_element_type=_F32)
        acc_ref[...] = jnp.concatenate([alpha, alpha], axis=1) * acc_ref[...] + pv
        m_ref[...] = m_new

    m_ref[...] = jnp.full(m_ref.shape, -jnp.inf, _F32)
    acc_ref[...] = jnp.zeros(acc_ref.shape, _F32)
    sc_ref[...] = scores(kc_ref[...])
    sa_ref[...] = scores(keys(0))
    process(sc_ref, vc_ref[...])

    def body(t, carry):
        b0 = 2 * t
        sb_ref[...] = scores(keys(b0 + 1))
        process(sa_ref, values(b0))
        sa_ref[...] = scores(keys(b0 + 2))
        process(sb_ref, values(b0 + 1))
        return carry

    lax.fori_loop(0, nk // 2 - 1, body, 0, unroll=ATTN_UNROLL)
    sb_ref[...] = scores(keys(nk - 1))
    process(sa_ref, values(nk - 2))
    process(sb_ref, values(nk - 1))

    og = acc_ref[:, 0:HEAD_DIM] / acc_ref[:, HEAD_DIM:2 * HEAD_DIM]
    for j in range(GROUP):
        o_ref[:, j * HEAD_DIM:(j + 1) * HEAD_DIM] = og[j * tq:(j + 1) * tq].astype(_BF16)


def _attention(q, kc, vc, k, v, casts):
    n, attn_w = q.shape
    c_len = kc.shape[0]
    gw = GROUP * HEAD_DIM
    assert n % TQ == 0 and n % (2 * TK) == 0 and c_len % LANES == 0
    rows = GROUP * TQ
    nq = n // TQ
    steps = N_KV_HEADS * nq
    kv = lambda a: pl.BlockSpec((a.shape[0], HEAD_DIM), lambda g, i: (0, g), pipeline_mode=pl.Buffered(1))

    in_slabs, out_slabs, out_shapes = [], [], []
    for w, chunk in casts:
        _, r, c = w.shape
        rep = next(m for m in range(1, steps + 1)
                   if steps % m == 0 and (r * m) % (steps * BF16_ROWS) == 0)
        sr = r * rep // steps
        in_slabs.append(pl.BlockSpec((None, sr, c), lambda g, i, rep=rep: (0, (g * nq + i) // rep, 0)))
        if chunk is None:
            out_slabs.append(pl.BlockSpec((sr, c), lambda g, i, rep=rep: ((g * nq + i) // rep, 0)))
            out_shapes.append(jax.ShapeDtypeStruct((r, c), _BF16))
        else:
            assert c % chunk == 0
            out_slabs.append(pl.BlockSpec((c // chunk, sr, chunk),
                                          lambda g, i, rep=rep: (0, (g * nq + i) // rep, 0)))
            out_shapes.append(jax.ShapeDtypeStruct((c // chunk, r, chunk), _BF16))

    outs = pl.pallas_call(
        _attn_kernel,
        grid=(N_KV_HEADS, nq),
        in_specs=[pl.BlockSpec((TQ, gw), lambda g, i: (i, g)),
                  kv(kc), kv(vc), kv(k), kv(v)] + in_slabs,
        out_specs=[pl.BlockSpec((TQ, gw), lambda g, i: (i, g))] + out_slabs,
        out_shape=[jax.ShapeDtypeStruct((n, attn_w), _BF16)] + out_shapes,
        scratch_shapes=[pltpu.VMEM((rows, LANES), _F32),
                        pltpu.VMEM((rows, 2 * HEAD_DIM), _F32),
                        pltpu.VMEM((rows, TK), _F32),
                        pltpu.VMEM((rows, TK), _F32),
                        pltpu.VMEM((rows, c_len), _F32)],
        compiler_params=_params(2),
        name="attention",
    )(q, kc, vc, k, v, *[w for w, _ in casts])
    return outs[0], outs[1:]


def _merge_kernel(a_ref, c_ref, aon_ref, w_ref, x_ref, g_ref, o_ref):
    aw = a_ref.shape[1]
    an = _rms(a_ref[...].astype(_F32), aon_ref[...]).astype(_BF16)
    y = jnp.dot(an, w_ref[0:aw, :], preferred_element_type=_F32)
    y = y + jnp.dot(c_ref[...], w_ref[aw:, :], preferred_element_type=_F32)
    o_ref[...] = x_ref[...] + g_ref[0:1, :] * y


def _merge(attn, conv_n, aon_w, w_o_b, x2d, mod):
    n, d = x2d.shape
    tm = TM_MERGE
    assert n % tm == 0
    row = lambda width: pl.BlockSpec((tm, width), lambda i: (i, 0))
    return pl.pallas_call(
        _merge_kernel,
        grid=(n // tm,),
        in_specs=[row(attn.shape[1]), row(conv_n.shape[1]),
                  pl.BlockSpec((1, attn.shape[1]), lambda i: (0, 0)),
                  pl.BlockSpec(w_o_b.shape, lambda i: (0, 0), pipeline_mode=pl.Buffered(1)),
                  row(d),
                  pl.BlockSpec((SUBLANES_F32, d), lambda i: (0, 2))],
        out_specs=row(d),
        out_shape=jax.ShapeDtypeStruct((n, d), _F32),
        compiler_params=_params(1),
        name="merge",
    )(attn, conv_n, aon_w, w_o_b, x2d, mod)


def _ffn_kernel(n_rows, xp_ref, xm_ref, xn_ref, n2_ref, sh_ref, sc_ref, g_ref, wa_ref, wg_ref,
                cwa_ref, cwg_ref, wd_ref, fn_ref, o_ref, h_ref):
    i = pl.program_id(0)
    j = pl.program_id(1)
    tm = xm_ref.shape[0]

    @pl.when(j == 0)
    def _():
        xe = jnp.concatenate([xp_ref[...], xm_ref[...], xn_ref[...]], axis=0)
        h = _rms(xe, n2_ref[...]) * (1.0 + sc_ref[0:1, :]) + sh_ref[0:1, :]
        h_ref[...] = jnp.where(_halo_valid(i, tm, n_rows), h, 0.0).astype(_BF16)
        o_ref[...] = jnp.zeros_like(o_ref)

    hb = h_ref[...]

    def conv(w_ref, cw_ref):
        u = jnp.dot(hb, w_ref[...], preferred_element_type=_F32)
        up, uc, un = _shift_rows(u, tm)
        return up * cw_ref[0:1, :] + uc * cw_ref[1:2, :] + un * cw_ref[2:3, :]

    a = conv(wa_ref, cwa_ref)
    g = conv(wg_ref, cwg_ref)
    act = (g / (1.0 + jnp.exp(-g))) * a
    o_ref[...] += jnp.dot(act.astype(_BF16), wd_ref[...], preferred_element_type=_F32)

    @pl.when(j == pl.num_programs(1) - 1)
    def _():
        xs = xm_ref[...] + g_ref[0:1, :] * o_ref[...]
        o_ref[...] = _rms(xs, fn_ref[...])


def _conv_ffn(xs, norm2_w, mod, w_up_b, ffn_conv_w, w_down_b, final_w):
    n, d = xs.shape
    d_ff = w_down_b.shape[0]
    tm, tf = TM_FFN, TF_FFN
    assert n % tm == 0 and d_ff % tf == 0 and tm % HALO == 0
    nf = d_ff // tf
    assert w_up_b.shape == (2 * nf, d, tf), "w_up must be chunk-major: (2*D_FF // tf, D, tf)"
    r = tm // HALO
    last_halo = n // HALO - 1
    vec = pl.BlockSpec((1, d), lambda i, j: (0, 0))
    modspec = lambda k: pl.BlockSpec((SUBLANES_F32, d), lambda i, j: (0, k))
    return pl.pallas_call(
        functools.partial(_ffn_kernel, n),
        grid=(n // tm, nf),
        in_specs=[pl.BlockSpec((HALO, d), lambda i, j: (jnp.maximum(i * r - 1, 0), 0)),
                  pl.BlockSpec((tm, d), lambda i, j: (i, 0), pipeline_mode=pl.Buffered(1)),
                  pl.BlockSpec((HALO, d), lambda i, j: (jnp.minimum((i + 1) * r, last_halo), 0)),
                  vec, modspec(3), modspec(4), modspec(5),
                  pl.BlockSpec((None, d, tf), lambda i, j: (j, 0, 0)),
                  pl.BlockSpec((None, d, tf), lambda i, j: (j + nf, 0, 0)),
                  pl.BlockSpec((CONV_K, tf), lambda i, j: (0, j)),
                  pl.BlockSpec((CONV_K, tf), lambda i, j: (0, j + nf)),
                  pl.BlockSpec((tf, d), lambda i, j: (j, 0)),
                  vec],
        out_specs=pl.BlockSpec((tm, d), lambda i, j: (i, 0)),
        out_shape=jax.ShapeDtypeStruct((n, d), _F32),
        scratch_shapes=[pltpu.VMEM((tm + 2 * HALO, d), _BF16)],
        compiler_params=_params(2),
        name="conv_ffn",
    )(xs, xs, xs, norm2_w, mod, mod, mod, w_up_b, w_up_b, ffn_conv_w, ffn_conv_w, w_down_b, final_w)


def kernel(x, c, ctx, c_ctx, w_ada, b_ada, norm1_w, w_in, q_norm_w, k_norm_w, conv_w, attn_out_norm_w,
           conv_out_norm_w, w_o, norm2_w, w_ffn_up, ffn_conv_w, w_ffn_down, final_norm_w):
    b, n, d = x.shape
    assert b == 1 and w_ada.shape[0] == 1, "single batch element, single trunk layer"
    x2d = x[0]
    row = lambda v: v.reshape(1, -1)

    mod = _adaln(c, c_ctx, w_ada[0], b_ada[0])
    cos, sin = _rope_tables(max(n // GRID_W, GRID_W))

    w_in_b = w_in[0].astype(_BF16)
    qn_scaled = row(q_norm_w[0]) * (HEAD_DIM ** -0.5 * math.log2(math.e))
    kc, vc = _ctx_kv(ctx[0], row(norm1_w[0]), mod, w_in_b, row(k_norm_w[0]))
    q, k, v, conv_n = _in_proj(x2d, row(norm1_w[0]), mod, w_in_b, qn_scaled, row(k_norm_w[0]),
                               cos, sin, conv_w[0], row(conv_out_norm_w[0]))
    attn, (w_o_b, w_up_b, w_down_b) = _attention(
        q, kc, vc, k, v, [(w_o, None), (w_ffn_up, TF_FFN), (w_ffn_down, None)])
    xs = _merge(attn, conv_n, row(attn_out_norm_w[0]), w_o_b, x2d, mod)
    out = _conv_ffn(xs, row(norm2_w[0]), mod, w_up_b, ffn_conv_w[0], w_down_b, row(final_norm_w))
    return out[None]
```

```python
import functools
import math

import jax
import jax.numpy as jnp
from jax import lax
from jax.experimental import pallas as pl
from jax.experimental.pallas import tpu as pltpu

GRID_W = 64
N_HEADS = 8
N_KV_HEADS = 2
HEAD_DIM = 128
GROUP = N_HEADS // N_KV_HEADS
ROPE_THETA = 10000.0
ROPE_AXIS_DIM = HEAD_DIM // 2
EPS = 1e-6
N_MOD = 6
CONV_K = 3

LANES = 128
SUBLANES_F32 = 8
BF16_ROWS = 16
VMEM_LIMIT_BYTES = 60 * 1024 * 1024

ADA_TN = 1024
TM_IN = 512
TQ = 256
TK = 1024
ATTN_UNROLL = True
TM_MERGE = 512
TM_FFN = 1024
TF_FFN = 512
HALO = SUBLANES_F32

_BF16 = jnp.bfloat16
_F32 = jnp.float32


def _params(n_axes):
    return pltpu.CompilerParams(dimension_semantics=("arbitrary",) * n_axes,
                                vmem_limit_bytes=VMEM_LIMIT_BYTES)


def _rms(x, w):
    return x * lax.rsqrt(jnp.mean(x * x, axis=-1, keepdims=True) + EPS) * w


def _norm_mod(x, norm_w, scale, shift):
    w = norm_w * (1.0 + scale)
    return x * lax.rsqrt(jnp.mean(x * x, axis=-1, keepdims=True) + EPS) * w + shift


def _halo_rows(i, n_tiles, xn_ref, xp_ref, norm_w, scale, shift):
    xh = jnp.concatenate([xn_ref[...], xp_ref[...]], axis=0)
    r = lax.broadcasted_iota(jnp.int32, (2 * HALO, 1), 0)
    has_next = (i < n_tiles - 1).astype(jnp.int32)
    has_prev = (i > 0).astype(jnp.int32)
    valid = jnp.where(r < HALO, has_next, has_prev) > 0
    return jnp.where(valid, _norm_mod(xh, norm_w, scale, shift), 0.0)


def _shift_rows(u, tm):
    rows = u.shape[0]
    prev = pltpu.roll(u, 1, axis=0)[0:tm]
    nxt = pltpu.roll(u, rows - 1, axis=0)[0:tm]
    return prev, u[0:tm], nxt


def _adaln_kernel(cb_ref, w_ref, b_ref, o_ref):
    rows = []
    for r in range(2):
        cv = cb_ref[r]
        s = cv / (1.0 + jnp.exp(-cv))
        parts = []
        for j in range(ADA_TN // LANES):
            wj = w_ref[:, j * LANES:(j + 1) * LANES]
            parts.append(jnp.sum(s * wj, axis=0, keepdims=True))
        rows.append(jnp.concatenate(parts, axis=1) + b_ref[...])
    pad = jnp.zeros((SUBLANES_F32 - 2, ADA_TN), _F32)
    o_ref[...] = jnp.concatenate(rows + [pad], axis=0)


def _adaln(c, c_ctx, w_ada, b_ada):
    d = w_ada.shape[0]
    n_out = w_ada.shape[1]
    cvec = jnp.stack([c[0], c_ctx], axis=0)
    cb = jnp.broadcast_to(cvec[:, :, None], (2, d, LANES))
    return pl.pallas_call(
        _adaln_kernel,
        grid=(n_out // ADA_TN,),
        in_specs=[pl.BlockSpec((2, d, LANES), lambda j: (0, 0, 0)),
                  pl.BlockSpec((d, ADA_TN), lambda j: (0, j)),
                  pl.BlockSpec((1, ADA_TN), lambda j: (0, j))],
        out_specs=pl.BlockSpec((SUBLANES_F32, ADA_TN), lambda j: (0, j)),
        out_shape=jax.ShapeDtypeStruct((SUBLANES_F32, n_out), _F32),
        compiler_params=_params(1),
        name="adaln",
    )(cb, w_ada, b_ada.reshape(1, n_out))


def _rope_kernel(freq_ref, sign_ref, cos_ref, sin_ref):
    pos = lax.broadcasted_iota(jnp.int32, cos_ref.shape, 0)
    ang = pos.astype(_F32) * freq_ref[...]
    cos_ref[...] = jnp.cos(ang)
    sin_ref[...] = jnp.sin(ang) * sign_ref[...]


def _rope_tables(n_pos):
    half = ROPE_AXIS_DIM // 2
    freqs = ROPE_THETA ** (-jnp.arange(half, dtype=_F32) / half)
    freq = jnp.tile(freqs, HEAD_DIM // half).reshape(1, HEAD_DIM)
    sign = jnp.tile(jnp.concatenate([-jnp.ones((half,), _F32), jnp.ones((half,), _F32)]),
                    HEAD_DIM // ROPE_AXIS_DIM).reshape(1, HEAD_DIM)
    vec = pl.BlockSpec((1, HEAD_DIM), lambda i: (0, 0))
    tab = pl.BlockSpec((n_pos, HEAD_DIM), lambda i: (0, 0))
    return pl.pallas_call(
        _rope_kernel,
        grid=(1,),
        in_specs=[vec, vec],
        out_specs=[tab, tab],
        out_shape=[jax.ShapeDtypeStruct((n_pos, HEAD_DIM), _F32)] * 2,
        compiler_params=_params(1),
        name="rope_tables",
    )(freq, sign)


def _rope_tile(row_ref, col_ref):
    lane = lax.broadcasted_iota(jnp.int32, (GRID_W, HEAD_DIM), 1)
    col = col_ref[...]
    parts = [jnp.where(lane < ROPE_AXIS_DIM, jnp.broadcast_to(row_ref[g:g + 1, :], (GRID_W, HEAD_DIM)), col)
             for g in range(row_ref.shape[0])]
    return jnp.concatenate(parts, axis=0)


def _rope_apply(x, cos, sin_signed):
    half = ROPE_AXIS_DIM // 2
    lane = lax.broadcasted_iota(jnp.int32, x.shape, 1)
    first = (lane & (ROPE_AXIS_DIM - 1)) < half
    swapped = jnp.where(first, pltpu.roll(x, HEAD_DIM - half, axis=1), pltpu.roll(x, half, axis=1))
    return x * cos + swapped * sin_signed


def _ctx_kv_kernel(x_ref, n1_ref, sh_ref, sc_ref, w_ref, kn_ref, k_ref, v_ref):
    x = x_ref[...]
    h = _rms(x, n1_ref[...]) * (1.0 + sc_ref[1:2, :]) + sh_ref[1:2, :]
    p = jnp.dot(h.astype(_BF16), w_ref[...], preferred_element_type=_F32)
    kvw = N_KV_HEADS * HEAD_DIM
    for g in range(N_KV_HEADS):
        kg = p[:, g * HEAD_DIM:(g + 1) * HEAD_DIM]
        k_ref[:, g * HEAD_DIM:(g + 1) * HEAD_DIM] = _rms(kg, kn_ref[...]).astype(_BF16)
    v_ref[...] = p[:, kvw:].astype(_BF16)


def _ctx_kv(ctx2d, norm1_w, mod, w_in_b, k_norm_w):
    c_len, d = ctx2d.shape
    attn_w = N_HEADS * HEAD_DIM
    kvw = N_KV_HEADS * HEAD_DIM
    assert attn_w % (2 * kvw) == 0
    return pl.pallas_call(
        _ctx_kv_kernel,
        grid=(1,),
        in_specs=[pl.BlockSpec((c_len, d), lambda i: (0, 0)),
                  pl.BlockSpec((1, d), lambda i: (0, 0)),
                  pl.BlockSpec((SUBLANES_F32, d), lambda i: (0, 0)),
                  pl.BlockSpec((SUBLANES_F32, d), lambda i: (0, 1)),
                  pl.BlockSpec((d, 2 * kvw), lambda i: (0, attn_w // (2 * kvw))),
                  pl.BlockSpec((1, HEAD_DIM), lambda i: (0, 0))],
        out_specs=[pl.BlockSpec((c_len, kvw), lambda i: (0, 0))] * 2,
        out_shape=[jax.ShapeDtypeStruct((c_len, kvw), _BF16)] * 2,
        compiler_params=_params(1),
        name="ctx_kv",
    )(ctx2d, norm1_w, mod, mod, w_in_b, k_norm_w)


def _in_proj_kernel(xp_ref, xm_ref, xn_ref, n1_ref, sh_ref, sc_ref, w_ref, qn_ref, kn_ref,
                    cos_row_ref, cos_col_ref, sin_row_ref, sin_col_ref, cw_ref, con_ref,
                    q_ref, k_ref, v_ref, conv_ref):
    i = pl.program_id(0)
    tm = xm_ref.shape[0]
    attn_w = N_HEADS * HEAD_DIM
    kvw = N_KV_HEADS * HEAD_DIM
    conv_w = conv_ref.shape[1]

    n1, sc, sh = n1_ref[...], sc_ref[0:1, :], sh_ref[0:1, :]
    hb = _norm_mod(xm_ref[...], n1, sc, sh).astype(_BF16)
    hb_halo = _halo_rows(i, pl.num_programs(0), xn_ref, xp_ref, n1, sc, sh).astype(_BF16)
    hb_ext = jnp.concatenate([hb, hb_halo], axis=0)
    cos = _rope_tile(cos_row_ref, cos_col_ref)
    sin = _rope_tile(sin_row_ref, sin_col_ref)

    def proj(lhs, lo, width):
        return jnp.dot(lhs, w_ref[:, lo:lo + width], preferred_element_type=_F32)

    step = 2 * HEAD_DIM
    for c0 in range(0, attn_w, step):
        pq = proj(hb, c0, step)
        for s in range(0, step, HEAD_DIM):
            qh = _rms(pq[:, s:s + HEAD_DIM], qn_ref[...])
            q_ref[:, c0 + s:c0 + s + HEAD_DIM] = _rope_apply(qh, cos, sin).astype(_BF16)
    pk = proj(hb, attn_w, kvw)
    for s in range(0, kvw, HEAD_DIM):
        kh = _rms(pk[:, s:s + HEAD_DIM], kn_ref[...])
        k_ref[:, s:s + HEAD_DIM] = _rope_apply(kh, cos, sin).astype(_BF16)
    v_ref[...] = proj(hb, attn_w + kvw, kvw).astype(_BF16)

    o = attn_w + 2 * kvw
    cb = proj(hb, o, conv_w)
    u = proj(hb_ext, o + conv_w, conv_w) * proj(hb_ext, o + 2 * conv_w, conv_w)
    up, uc, un = _shift_rows(u, tm)
    y = cb * (up * cw_ref[0:1, :] + uc * cw_ref[1:2, :] + un * cw_ref[2:3, :])
    conv_ref[...] = _rms(y, con_ref[...]).astype(_BF16)


def _in_proj(x2d, norm1_w, mod, w_in_b, qn_scaled, k_norm_w, cos, sin, conv_w, con_w):
    n, d = x2d.shape
    tm = TM_IN
    assert n % tm == 0 and tm % HALO == 0
    assert tm == SUBLANES_F32 * GRID_W and cos.shape[0] >= max(n // GRID_W, GRID_W)
    rope_row = pl.BlockSpec((SUBLANES_F32, HEAD_DIM), lambda i: (i, 0))
    rope_col = pl.BlockSpec((GRID_W, HEAD_DIM), lambda i: (0, 0))
    attn_w = N_HEADS * HEAD_DIM
    kvw = N_KV_HEADS * HEAD_DIM
    cw = conv_w.shape[1]
    r = tm // HALO
    last_halo = n // HALO - 1
    vec = lambda width: pl.BlockSpec((1, width), lambda i: (0, 0))
    row = lambda width: pl.BlockSpec((tm, width), lambda i: (i, 0))
    return pl.pallas_call(
        _in_proj_kernel,
        grid=(n // tm,),
        in_specs=[pl.BlockSpec((HALO, d), lambda i: (jnp.maximum(i * r - 1, 0), 0)),
                  row(d),
                  pl.BlockSpec((HALO, d), lambda i: (jnp.minimum((i + 1) * r, last_halo), 0)),
                  vec(d),
                  pl.BlockSpec((SUBLANES_F32, d), lambda i: (0, 0)),
                  pl.BlockSpec((SUBLANES_F32, d), lambda i: (0, 1)),
                  pl.BlockSpec(w_in_b.shape, lambda i: (0, 0), pipeline_mode=pl.Buffered(1)),
                  vec(HEAD_DIM), vec(HEAD_DIM),
                  rope_row, rope_col, rope_row, rope_col,
                  pl.BlockSpec((CONV_K, cw), lambda i: (0, 0)),
                  vec(cw)],
        out_specs=[row(attn_w), row(kvw), row(kvw), row(cw)],
        out_shape=[jax.ShapeDtypeStruct((n, attn_w), _BF16),
                   jax.ShapeDtypeStruct((n, kvw), _BF16),
                   jax.ShapeDtypeStruct((n, kvw), _BF16),
                   jax.ShapeDtypeStruct((n, cw), _BF16)],
        compiler_params=_params(1),
        name="in_proj",
    )(x2d, x2d, x2d, norm1_w, mod, mod, w_in_b, qn_scaled, k_norm_w, cos, cos, sin, sin, conv_w, con_w)


def _attn_kernel(q_ref, kc_ref, vc_ref, k_ref, v_ref, w1_ref, w2_ref, w3_ref,
                 o_ref, w1b_ref, w2b_ref, w3b_ref, m_ref, acc_ref, sa_ref, sb_ref, sc_ref):
    for src, dst in ((w1_ref, w1b_ref), (w2_ref, w2b_ref), (w3_ref, w3b_ref)):
        dst[...] = src[...].astype(_BF16)
    tq = q_ref.shape[0]
    nk = k_ref.shape[0] // TK
    qg = jnp.concatenate([q_ref[:, j * HEAD_DIM:(j + 1) * HEAD_DIM] for j in range(GROUP)], axis=0)

    def scores(kb):
        return lax.dot_general(qg, kb, (((1,), (1,)), ((), ())), preferred_element_type=_F32)

    def keys(b):
        return k_ref[pl.ds(pl.multiple_of(b * TK, TK), TK), :]

    def values(b):
        return v_ref[pl.ds(pl.multiple_of(b * TK, TK), TK), :]

    def process(s_ref, vb):
        tk = vb.shape[0]
        sc = [s_ref[:, c * LANES:(c + 1) * LANES] for c in range(tk // LANES)]
        mc = functools.reduce(jnp.maximum, sc)
        m_prev = m_ref[...]
        m_new = jnp.maximum(m_prev, jnp.max(mc, axis=1, keepdims=True))
        alpha = jnp.exp2(m_prev - m_new)
        p = jnp.concatenate([jnp.exp2(x - m_new).astype(_BF16) for x in sc], axis=1)
        vb_ones = jnp.concatenate([vb, jnp.ones((tk, HEAD_DIM), _BF16)], axis=1)
        pv = jnp.dot(p, vb_ones, preferred_element_type=_F32)
        acc_ref[...] = jnp.concatenate([alpha, alpha], axis=1) * acc_ref[...] + pv
        m_ref[...] = m_new

    m_ref[...] = jnp.full(m_ref.shape, -jnp.inf, _F32)
    acc_ref[...] = jnp.zeros(acc_ref.shape, _F32)
    sc_ref[...] = scores(kc_ref[...])
    sa_ref[...] = scores(keys(0))
    process(sc_ref, vc_ref[...])

    def body(t, carry):
        b0 = 2 * t
        sb_ref[...] = scores(keys(b0 + 1))
        process(sa_ref, values(b0))
        sa_ref[...] = scores(keys(b0 + 2))
        process(sb_ref, values(b0 + 1))
        return carry

    lax.fori_loop(0, nk // 2 - 1, body, 0, unroll=ATTN_UNROLL)
    sb_ref[...] = scores(keys(nk - 1))
    process(sa_ref, values(nk - 2))
    process(sb_ref, values(nk - 1))

    og = acc_ref[:, 0:HEAD_DIM] / acc_ref[:, HEAD_DIM:2 * HEAD_DIM]
    for j in range(GROUP):
        o_ref[:, j * HEAD_DIM:(j + 1) * HEAD_DIM] = og[j * tq:(j + 1) * tq].astype(_BF16)


def _attention(q, kc, vc, k, v, f32_weights):
    n, attn_w = q.shape
    c_len = kc.shape[0]
    gw = GROUP * HEAD_DIM
    assert n % TQ == 0 and n % (2 * TK) == 0 and c_len % LANES == 0
    rows = GROUP * TQ
    nq = n // TQ
    steps = N_KV_HEADS * nq
    kv = lambda a: pl.BlockSpec((a.shape[0], HEAD_DIM), lambda g, i: (0, g), pipeline_mode=pl.Buffered(1))

    in_slabs, out_slabs, out_shapes = [], [], []
    for w in f32_weights:
        _, r, c = w.shape
        rep = next(m for m in range(1, steps + 1)
                   if steps % m == 0 and (r * m) % (steps * BF16_ROWS) == 0)
        sr = r * rep // steps
        in_slabs.append(pl.BlockSpec((None, sr, c), lambda g, i, rep=rep: (0, (g * nq + i) // rep, 0)))
        out_slabs.append(pl.BlockSpec((sr, c), lambda g, i, rep=rep: ((g * nq + i) // rep, 0)))
        out_shapes.append(jax.ShapeDtypeStruct((r, c), _BF16))

    outs = pl.pallas_call(
        _attn_kernel,
        grid=(N_KV_HEADS, nq),
        in_specs=[pl.BlockSpec((TQ, gw), lambda g, i: (i, g)),
                  kv(kc), kv(vc), kv(k), kv(v)] + in_slabs,
        out_specs=[pl.BlockSpec((TQ, gw), lambda g, i: (i, g))] + out_slabs,
        out_shape=[jax.ShapeDtypeStruct((n, attn_w), _BF16)] + out_shapes,
        scratch_shapes=[pltpu.VMEM((rows, LANES), _F32),
                        pltpu.VMEM((rows, 2 * HEAD_DIM), _F32),
                        pltpu.VMEM((rows, TK), _F32),
                        pltpu.VMEM((rows, TK), _F32),
                        pltpu.VMEM((rows, c_len), _F32)],
        compiler_params=_params(2),
        name="attention",
    )(q, kc, vc, k, v, *f32_weights)
    return outs[0], outs[1:]


def _merge_kernel(a_ref, c_ref, aon_ref, w_ref, x_ref, g_ref, o_ref):
    aw = a_ref.shape[1]
    an = _rms(a_ref[...].astype(_F32), aon_ref[...]).astype(_BF16)
    y = jnp.dot(an, w_ref[0:aw, :], preferred_element_type=_F32)
    y = y + jnp.dot(c_ref[...], w_ref[aw:, :], preferred_element_type=_F32)
    o_ref[...] = x_ref[...] + g_ref[0:1, :] * y


def _merge(attn, conv_n, aon_w, w_o_b, x2d, mod):
    n, d = x2d.shape
    tm = TM_MERGE
    assert n % tm == 0
    row = lambda width: pl.BlockSpec((tm, width), lambda i: (i, 0))
    return pl.pallas_call(
        _merge_kernel,
        grid=(n // tm,),
        in_specs=[row(attn.shape[1]), row(conv_n.shape[1]),
                  pl.BlockSpec((1, attn.shape[1]), lambda i: (0, 0)),
                  pl.BlockSpec(w_o_b.shape, lambda i: (0, 0), pipeline_mode=pl.Buffered(1)),
                  row(d),
                  pl.BlockSpec((SUBLANES_F32, d), lambda i: (0, 2))],
        out_specs=row(d),
        out_shape=jax.ShapeDtypeStruct((n, d), _F32),
        compiler_params=_params(1),
        name="merge",
    )(attn, conv_n, aon_w, w_o_b, x2d, mod)


def _ffn_kernel(xp_ref, xm_ref, xn_ref, n2_ref, sh_ref, sc_ref, g_ref, wa_ref, wg_ref,
                cwa_ref, cwg_ref, wd_ref, fn_ref, o_ref, h_ref):
    i = pl.program_id(0)
    j = pl.program_id(1)
    tm = xm_ref.shape[0]

    @pl.when(j == 0)
    def _():
        n2, sc, sh = n2_ref[...], sc_ref[0:1, :], sh_ref[0:1, :]
        h_ref[0:tm, :] = _norm_mod(xm_ref[...], n2, sc, sh).astype(_BF16)
        h_ref[tm:, :] = _halo_rows(i, pl.num_programs(0), xn_ref, xp_ref, n2, sc, sh).astype(_BF16)
        o_ref[...] = jnp.zeros_like(o_ref)

    hb = h_ref[...]

    def conv(w_ref, cw_ref):
        u = jnp.dot(hb, w_ref[...], preferred_element_type=_F32)
        up, uc, un = _shift_rows(u, tm)
        return up * cw_ref[0:1, :] + uc * cw_ref[1:2, :] + un * cw_ref[2:3, :]

    half_g = 0.5 * conv(wg_ref, cwg_ref)
    silu_g = half_g + half_g * jnp.tanh(half_g)
    act = silu_g * conv(wa_ref, cwa_ref)
    o_ref[...] += jnp.dot(act.astype(_BF16), wd_ref[...], preferred_element_type=_F32)

    @pl.when(j == pl.num_programs(1) - 1)
    def _():
        xs = xm_ref[...] + g_ref[0:1, :] * o_ref[...]
        o_ref[...] = _rms(xs, fn_ref[...])


def _conv_ffn(xs, norm2_w, mod, w_up_b, ffn_conv_w, w_down_b, final_w):
    n, d = xs.shape
    d_ff = w_down_b.shape[0]
    tm, tf = TM_FFN, TF_FFN
    assert n % tm == 0 and d_ff % tf == 0 and tm % HALO == 0
    nf = d_ff // tf
    r = tm // HALO
    last_halo = n // HALO - 1
    vec = pl.BlockSpec((1, d), lambda i, j: (0, 0))
    modspec = lambda k: pl.BlockSpec((SUBLANES_F32, d), lambda i, j: (0, k))
    return pl.pallas_call(
        _ffn_kernel,
        grid=(n // tm, nf),
        in_specs=[pl.BlockSpec((HALO, d), lambda i, j: (jnp.maximum(i * r - 1, 0), 0)),
                  pl.BlockSpec((tm, d), lambda i, j: (i, 0), pipeline_mode=pl.Buffered(1)),
                  pl.BlockSpec((HALO, d), lambda i, j: (jnp.minimum((i + 1) * r, last_halo), 0)),
                  vec, modspec(3), modspec(4), modspec(5),
                  pl.BlockSpec((d, tf), lambda i, j: (0, j)),
                  pl.BlockSpec((d, tf), lambda i, j: (0, j + nf)),
                  pl.BlockSpec((CONV_K, tf), lambda i, j: (0, j)),
                  pl.BlockSpec((CONV_K, tf), lambda i, j: (0, j + nf)),
                  pl.BlockSpec((tf, d), lambda i, j: (j, 0)),
                  vec],
        out_specs=pl.BlockSpec((tm, d), lambda i, j: (i, 0)),
        out_shape=jax.ShapeDtypeStruct((n, d), _F32),
        scratch_shapes=[pltpu.VMEM((tm + 2 * HALO, d), _BF16)],
        compiler_params=_params(2),
        name="conv_ffn",
    )(xs, xs, xs, norm2_w, mod, mod, mod, w_up_b, w_up_b, ffn_conv_w, ffn_conv_w, w_down_b, final_w)


def kernel(x, c, ctx, c_ctx, w_ada, b_ada, norm1_w, w_in, q_norm_w, k_norm_w, conv_w, attn_out_norm_w,
           conv_out_norm_w, w_o, norm2_w, w_ffn_up, ffn_conv_w, w_ffn_down, final_norm_w):
    b, n, d = x.shape
    assert b == 1 and w_ada.shape[0] == 1, "single batch element, single trunk layer"
    x2d = x[0]
    row = lambda v: v.reshape(1, -1)

    mod = _adaln(c, c_ctx, w_ada[0], b_ada[0])
    cos, sin = _rope_tables(max(n // GRID_W, GRID_W))

    w_in_b = w_in[0].astype(_BF16)
    qn_scaled = row(q_norm_w[0]) * (HEAD_DIM ** -0.5 * math.log2(math.e))
    kc, vc = _ctx_kv(ctx[0], row(norm1_w[0]), mod, w_in_b, row(k_norm_w[0]))
    q, k, v, conv_n = _in_proj(x2d, row(norm1_w[0]), mod, w_in_b, qn_scaled, row(k_norm_w[0]),
                               cos, sin, conv_w[0], row(conv_out_norm_w[0]))
    attn, (w_o_b, w_up_b, w_down_b) = _attention(
        q, kc, vc, k, v, [w_o, w_ffn_up, w_ffn_down])
    xs = _merge(attn, conv_n, row(attn_out_norm_w[0]), w_o_b, x2d, mod)
    out = _conv_ffn(xs, row(norm2_w[0]), mod, w_up_b, ffn_conv_w[0], w_down_b, row(final_norm_w))
    return out[None]
```

```python
import functools
import math

import jax
import jax.numpy as jnp
from jax import lax
from jax.experimental import pallas as pl
from jax.experimental.pallas import tpu as pltpu

GRID_W = 64
N_HEADS = 8
N_KV_HEADS = 2
HEAD_DIM = 128
GROUP = N_HEADS // N_KV_HEADS
ROPE_THETA = 10000.0
ROPE_AXIS_DIM = HEAD_DIM // 2
EPS = 1e-6
N_MOD = 6
CONV_K = 3

LANES = 128
SUBLANES_F32 = 8
BF16_ROWS = 16
VMEM_LIMIT_BYTES = 60 * 1024 * 1024

ADA_TN = 1024
TM_IN = 512
TQ = 256
TK = 2048
TM_MERGE = 512
TM_FFN = 1024
TF_FFN = 512
HALO = SUBLANES_F32

_BF16 = jnp.bfloat16
_F32 = jnp.float32


def _params(n_axes):
    return pltpu.CompilerParams(dimension_semantics=("arbitrary",) * n_axes,
                                vmem_limit_bytes=VMEM_LIMIT_BYTES)


def _rms(x, w):
    return x * lax.rsqrt(jnp.mean(x * x, axis=-1, keepdims=True) + EPS) * w


def _norm_mod(x, norm_w, scale, shift):
    w = norm_w * (1.0 + scale)
    return x * lax.rsqrt(jnp.mean(x * x, axis=-1, keepdims=True) + EPS) * w + shift


def _halo_rows(i, n_tiles, xn_ref, xp_ref, norm_w, scale, shift):
    xh = jnp.concatenate([xn_ref[...], xp_ref[...]], axis=0)
    r = lax.broadcasted_iota(jnp.int32, (2 * HALO, 1), 0)
    has_next = (i < n_tiles - 1).astype(jnp.int32)
    has_prev = (i > 0).astype(jnp.int32)
    valid = jnp.where(r < HALO, has_next, has_prev) > 0
    return jnp.where(valid, _norm_mod(xh, norm_w, scale, shift), 0.0)


def _shift_rows(u, tm):
    rows = u.shape[0]
    prev = pltpu.roll(u, 1, axis=0)[0:tm]
    nxt = pltpu.roll(u, rows - 1, axis=0)[0:tm]
    return prev, u[0:tm], nxt


def _adaln_kernel(cb_ref, w_ref, b_ref, o_ref):
    rows = []
    for r in range(2):
        cv = cb_ref[r]
        s = cv / (1.0 + jnp.exp(-cv))
        parts = []
        for j in range(ADA_TN // LANES):
            wj = w_ref[:, j * LANES:(j + 1) * LANES]
            parts.append(jnp.sum(s * wj, axis=0, keepdims=True))
        rows.append(jnp.concatenate(parts, axis=1) + b_ref[...])
    pad = jnp.zeros((SUBLANES_F32 - 2, ADA_TN), _F32)
    o_ref[...] = jnp.concatenate(rows + [pad], axis=0)


def _adaln(c, c_ctx, w_ada, b_ada):
    d = w_ada.shape[0]
    n_out = w_ada.shape[1]
    cvec = jnp.stack([c[0], c_ctx], axis=0)
    cb = jnp.broadcast_to(cvec[:, :, None], (2, d, LANES))
    return pl.pallas_call(
        _adaln_kernel,
        grid=(n_out // ADA_TN,),
        in_specs=[pl.BlockSpec((2, d, LANES), lambda j: (0, 0, 0)),
                  pl.BlockSpec((d, ADA_TN), lambda j: (0, j)),
                  pl.BlockSpec((1, ADA_TN), lambda j: (0, j))],
        out_specs=pl.BlockSpec((SUBLANES_F32, ADA_TN), lambda j: (0, j)),
        out_shape=jax.ShapeDtypeStruct((SUBLANES_F32, n_out), _F32),
        compiler_params=_params(1),
        name="adaln",
    )(cb, w_ada, b_ada.reshape(1, n_out))


def _rope_kernel(freq_ref, sign_ref, cos_ref, sin_ref):
    pos = lax.broadcasted_iota(jnp.int32, cos_ref.shape, 0)
    ang = pos.astype(_F32) * freq_ref[...]
    cos_ref[...] = jnp.cos(ang)
    sin_ref[...] = jnp.sin(ang) * sign_ref[...]


def _rope_tables(n_pos):
    half = ROPE_AXIS_DIM // 2
    freqs = ROPE_THETA ** (-jnp.arange(half, dtype=_F32) / half)
    freq = jnp.tile(freqs, HEAD_DIM // half).reshape(1, HEAD_DIM)
    sign = jnp.tile(jnp.concatenate([-jnp.ones((half,), _F32), jnp.ones((half,), _F32)]),
                    HEAD_DIM // ROPE_AXIS_DIM).reshape(1, HEAD_DIM)
    vec = pl.BlockSpec((1, HEAD_DIM), lambda i: (0, 0))
    tab = pl.BlockSpec((n_pos, HEAD_DIM), lambda i: (0, 0))
    return pl.pallas_call(
        _rope_kernel,
        grid=(1,),
        in_specs=[vec, vec],
        out_specs=[tab, tab],
        out_shape=[jax.ShapeDtypeStruct((n_pos, HEAD_DIM), _F32)] * 2,
        compiler_params=_params(1),
        name="rope_tables",
    )(freq, sign)


def _rope_tile(row_ref, col_ref):
    lane = lax.broadcasted_iota(jnp.int32, (GRID_W, HEAD_DIM), 1)
    col = col_ref[...]
    parts = [jnp.where(lane < ROPE_AXIS_DIM, jnp.broadcast_to(row_ref[g:g + 1, :], (GRID_W, HEAD_DIM)), col)
             for g in range(row_ref.shape[0])]
    return jnp.concatenate(parts, axis=0)


def _rope_apply(x, cos, sin_signed):
    half = ROPE_AXIS_DIM // 2
    lane = lax.broadcasted_iota(jnp.int32, x.shape, 1)
    first = (lane & (ROPE_AXIS_DIM - 1)) < half
    swapped = jnp.where(first, pltpu.roll(x, HEAD_DIM - half, axis=1), pltpu.roll(x, half, axis=1))
    return x * cos + swapped * sin_signed


def _ctx_kv_kernel(x_ref, n1_ref, sh_ref, sc_ref, w_ref, kn_ref, k_ref, v_ref):
    x = x_ref[...]
    h = _rms(x, n1_ref[...]) * (1.0 + sc_ref[1:2, :]) + sh_ref[1:2, :]
    p = jnp.dot(h.astype(_BF16), w_ref[...], preferred_element_type=_F32)
    kvw = N_KV_HEADS * HEAD_DIM
    for g in range(N_KV_HEADS):
        kg = p[:, g * HEAD_DIM:(g + 1) * HEAD_DIM]
        k_ref[:, g * HEAD_DIM:(g + 1) * HEAD_DIM] = _rms(kg, kn_ref[...]).astype(_BF16)
    v_ref[...] = p[:, kvw:].astype(_BF16)


def _ctx_kv(ctx2d, norm1_w, mod, w_in_b, k_norm_w):
    c_len, d = ctx2d.shape
    attn_w = N_HEADS * HEAD_DIM
    kvw = N_KV_HEADS * HEAD_DIM
    assert attn_w % (2 * kvw) == 0
    return pl.pallas_call(
        _ctx_kv_kernel,
        grid=(1,),
        in_specs=[pl.BlockSpec((c_len, d), lambda i: (0, 0)),
                  pl.BlockSpec((1, d), lambda i: (0, 0)),
                  pl.BlockSpec((SUBLANES_F32, d), lambda i: (0, 0)),
                  pl.BlockSpec((SUBLANES_F32, d), lambda i: (0, 1)),
                  pl.BlockSpec((d, 2 * kvw), lambda i: (0, attn_w // (2 * kvw))),
                  pl.BlockSpec((1, HEAD_DIM), lambda i: (0, 0))],
        out_specs=[pl.BlockSpec((c_len, kvw), lambda i: (0, 0))] * 2,
        out_shape=[jax.ShapeDtypeStruct((c_len, kvw), _BF16)] * 2,
        compiler_params=_params(1),
        name="ctx_kv",
    )(ctx2d, norm1_w, mod, mod, w_in_b, k_norm_w)


def _in_proj_kernel(xp_ref, xm_ref, xn_ref, n1_ref, sh_ref, sc_ref, w_ref, qn_ref, kn_ref,
                    cos_row_ref, cos_col_ref, sin_row_ref, sin_col_ref, cw_ref, con_ref,
                    q_ref, k_ref, v_ref, conv_ref):
    i = pl.program_id(0)
    tm = xm_ref.shape[0]
    attn_w = N_HEADS * HEAD_DIM
    kvw = N_KV_HEADS * HEAD_DIM
    conv_w = conv_ref.shape[1]

    n1, sc, sh = n1_ref[...], sc_ref[0:1, :], sh_ref[0:1, :]
    hb = _norm_mod(xm_ref[...], n1, sc, sh).astype(_BF16)
    hb_halo = _halo_rows(i, pl.num_programs(0), xn_ref, xp_ref, n1, sc, sh).astype(_BF16)
    hb_ext = jnp.concatenate([hb, hb_halo], axis=0)
    cos = _rope_tile(cos_row_ref, cos_col_ref)
    sin = _rope_tile(sin_row_ref, sin_col_ref)

    def proj(lhs, lo, width):
        return jnp.dot(lhs, w_ref[:, lo:lo + width], preferred_element_type=_F32)

    step = 2 * HEAD_DIM
    for c0 in range(0, attn_w, step):
        pq = proj(hb, c0, step)
        for s in range(0, step, HEAD_DIM):
            qh = _rms(pq[:, s:s + HEAD_DIM], qn_ref[...])
            q_ref[:, c0 + s:c0 + s + HEAD_DIM] = _rope_apply(qh, cos, sin).astype(_BF16)
    pk = proj(hb, attn_w, kvw)
    for s in range(0, kvw, HEAD_DIM):
        kh = _rms(pk[:, s:s + HEAD_DIM], kn_ref[...])
        k_ref[:, s:s + HEAD_DIM] = _rope_apply(kh, cos, sin).astype(_BF16)
    v_ref[...] = proj(hb, attn_w + kvw, kvw).astype(_BF16)

    o = attn_w + 2 * kvw
    u = proj(hb_ext, o + conv_w, conv_w) * proj(hb_ext, o + 2 * conv_w, conv_w)
    up, uc, un = _shift_rows(u, tm)
    conv = up * cw_ref[0:1, :] + uc * cw_ref[1:2, :] + un * cw_ref[2:3, :]
    y = proj(hb, o, conv_w) * conv
    conv_ref[...] = _rms(y, con_ref[...]).astype(_BF16)


def _in_proj(x2d, norm1_w, mod, w_in_b, qn_scaled, k_norm_w, cos, sin, conv_w, con_w):
    n, d = x2d.shape
    tm = TM_IN
    assert n % tm == 0 and tm % HALO == 0
    assert tm == SUBLANES_F32 * GRID_W and cos.shape[0] >= max(n // GRID_W, GRID_W)
    rope_row = pl.BlockSpec((SUBLANES_F32, HEAD_DIM), lambda i: (i, 0))
    rope_col = pl.BlockSpec((GRID_W, HEAD_DIM), lambda i: (0, 0))
    attn_w = N_HEADS * HEAD_DIM
    kvw = N_KV_HEADS * HEAD_DIM
    cw = conv_w.shape[1]
    r = tm // HALO
    last_halo = n // HALO - 1
    vec = lambda width: pl.BlockSpec((1, width), lambda i: (0, 0))
    row = lambda width: pl.BlockSpec((tm, width), lambda i: (i, 0))
    return pl.pallas_call(
        _in_proj_kernel,
        grid=(n // tm,),
        in_specs=[pl.BlockSpec((HALO, d), lambda i: (jnp.maximum(i * r - 1, 0), 0)),
                  row(d),
                  pl.BlockSpec((HALO, d), lambda i: (jnp.minimum((i + 1) * r, last_halo), 0)),
                  vec(d),
                  pl.BlockSpec((SUBLANES_F32, d), lambda i: (0, 0)),
                  pl.BlockSpec((SUBLANES_F32, d), lambda i: (0, 1)),
                  pl.BlockSpec(w_in_b.shape, lambda i: (0, 0), pipeline_mode=pl.Buffered(1)),
                  vec(HEAD_DIM), vec(HEAD_DIM),
                  rope_row, rope_col, rope_row, rope_col,
                  pl.BlockSpec((CONV_K, cw), lambda i: (0, 0)),
                  vec(cw)],
        out_specs=[row(attn_w), row(kvw), row(kvw), row(cw)],
        out_shape=[jax.ShapeDtypeStruct((n, attn_w), _BF16),
                   jax.ShapeDtypeStruct((n, kvw), _BF16),
                   jax.ShapeDtypeStruct((n, kvw), _BF16),
                   jax.ShapeDtypeStruct((n, cw), _BF16)],
        compiler_params=_params(1),
        name="in_proj",
    )(x2d, x2d, x2d, norm1_w, mod, mod, w_in_b, qn_scaled, k_norm_w, cos, cos, sin, sin, conv_w, con_w)


def _attn_kernel(q_ref, kc_ref, vc_ref, k_ref, v_ref, w1_ref, w2_ref, w3_ref,
                 o_ref, w1b_ref, w2b_ref, w3b_ref, m_ref, acc_ref, sa_ref, sb_ref):
    for src, dst in ((w1_ref, w1b_ref), (w2_ref, w2b_ref), (w3_ref, w3b_ref)):
        dst[...] = src[...].astype(_BF16)
    tq = q_ref.shape[0]
    nk = k_ref.shape[0] // TK
    qg = jnp.concatenate([q_ref[:, j * HEAD_DIM:(j + 1) * HEAD_DIM] for j in range(GROUP)], axis=0)

    def scores(kb):
        return lax.dot_general(qg, kb, (((1,), (1,)), ((), ())), preferred_element_type=_F32)


    def process(s_ref, vb):
        tk = vb.shape[0]
        sc = [s_ref[:, c * LANES:(c + 1) * LANES] for c in range(tk // LANES)]
        mc = functools.reduce(jnp.maximum, sc)
        m_prev = m_ref[...]
        m_new = jnp.maximum(m_prev, jnp.max(mc, axis=1, keepdims=True))
        alpha = jnp.exp2(m_prev - m_new)
        p = jnp.concatenate([jnp.exp2(x - m_new).astype(_BF16) for x in sc], axis=1)
        vb_ones = jnp.concatenate([vb, jnp.ones((tk, HEAD_DIM), _BF16)], axis=1)
        pv = jnp.dot(p, vb_ones, preferred_element_type=_F32)
        acc_ref[...] = jnp.concatenate([alpha, alpha], axis=1) * acc_ref[...] + pv
        m_ref[...] = m_new

    m_ref[...] = jnp.full(m_ref.shape, -jnp.inf, _F32)
    acc_ref[...] = jnp.zeros(acc_ref.shape, _F32)
    blocks = [(k_ref, v_ref, b * TK, TK) for b in range(nk)] + [(kc_ref, vc_ref, 0, kc_ref.shape[0])]
    bufs = (sa_ref, sb_ref)

    def put_scores(i):
        kr, _, lo, size = blocks[i]
        bufs[i % 2][:, 0:size] = scores(kr[lo:lo + size, :])

    put_scores(0)
    for i, (_, vr, lo, size) in enumerate(blocks):
        if i + 1 < len(blocks):
            put_scores(i + 1)
        process(bufs[i % 2], vr[lo:lo + size, :])

    og = acc_ref[:, 0:HEAD_DIM] / acc_ref[:, HEAD_DIM:2 * HEAD_DIM]
    for j in range(GROUP):
        o_ref[:, j * HEAD_DIM:(j + 1) * HEAD_DIM] = og[j * tq:(j + 1) * tq].astype(_BF16)


def _attention(q, kc, vc, k, v, f32_weights):
    n, attn_w = q.shape
    c_len = kc.shape[0]
    gw = GROUP * HEAD_DIM
    assert n % TQ == 0 and n % TK == 0 and c_len % LANES == 0 and c_len <= TK
    rows = GROUP * TQ
    nq = n // TQ
    steps = N_KV_HEADS * nq
    kv = lambda a: pl.BlockSpec((a.shape[0], HEAD_DIM), lambda g, i: (0, g), pipeline_mode=pl.Buffered(1))

    in_slabs, out_slabs, out_shapes = [], [], []
    for w in f32_weights:
        _, r, c = w.shape
        rep = next(m for m in range(1, steps + 1)
                   if steps % m == 0 and (r * m) % (steps * BF16_ROWS) == 0)
        sr = r * rep // steps
        in_slabs.append(pl.BlockSpec((None, sr, c), lambda g, i, rep=rep: (0, (g * nq + i) // rep, 0)))
        out_slabs.append(pl.BlockSpec((sr, c), lambda g, i, rep=rep: ((g * nq + i) // rep, 0)))
        out_shapes.append(jax.ShapeDtypeStruct((r, c), _BF16))

    outs = pl.pallas_call(
        _attn_kernel,
        grid=(N_KV_HEADS, nq),
        in_specs=[pl.BlockSpec((TQ, gw), lambda g, i: (i, g)),
                  kv(kc), kv(vc), kv(k), kv(v)] + in_slabs,
        out_specs=[pl.BlockSpec((TQ, gw), lambda g, i: (i, g))] + out_slabs,
        out_shape=[jax.ShapeDtypeStruct((n, attn_w), _BF16)] + out_shapes,
        scratch_shapes=[pltpu.VMEM((rows, LANES), _F32),
                        pltpu.VMEM((rows, 2 * HEAD_DIM), _F32),
                        pltpu.VMEM((rows, TK), _F32),
                        pltpu.VMEM((rows, TK), _F32)],
        compiler_params=_params(2),
        name="attention",
    )(q, kc, vc, k, v, *f32_weights)
    return outs[0], outs[1:]


def _merge_kernel(a_ref, c_ref, aon_ref, w_ref, x_ref, g_ref, o_ref):
    aw = a_ref.shape[1]
    an = _rms(a_ref[...].astype(_F32), aon_ref[...]).astype(_BF16)
    y = jnp.dot(an, w_ref[0:aw, :], preferred_element_type=_F32)
    y = y + jnp.dot(c_ref[...], w_ref[aw:, :], preferred_element_type=_F32)
    o_ref[...] = x_ref[...] + g_ref[0:1, :] * y


def _merge(attn, conv_n, aon_w, w_o_b, x2d, mod):
    n, d = x2d.shape
    tm = TM_MERGE
    assert n % tm == 0
    row = lambda width: pl.BlockSpec((tm, width), lambda i: (i, 0))
    return pl.pallas_call(
        _merge_kernel,
        grid=(n // tm,),
        in_specs=[row(attn.shape[1]), row(conv_n.shape[1]),
                  pl.BlockSpec((1, attn.shape[1]), lambda i: (0, 0)),
                  pl.BlockSpec(w_o_b.shape, lambda i: (0, 0), pipeline_mode=pl.Buffered(1)),
                  row(d),
                  pl.BlockSpec((SUBLANES_F32, d), lambda i: (0, 2))],
        out_specs=row(d),
        out_shape=jax.ShapeDtypeStruct((n, d), _F32),
        compiler_params=_params(1),
        name="merge",
    )(attn, conv_n, aon_w, w_o_b, x2d, mod)


def _ffn_kernel(xp_ref, xm_ref, xn_ref, n2_ref, sh_ref, sc_ref, g_ref, wa_ref, wg_ref,
                cwa_ref, cwg_ref, wd_ref, fn_ref, o_ref, h_ref):
    i = pl.program_id(0)
    j = pl.program_id(1)
    tm = xm_ref.shape[0]

    @pl.when(j == 0)
    def _():
        n2, sc, sh = n2_ref[...], sc_ref[0:1, :], sh_ref[0:1, :]
        h_ref[0:tm, :] = _norm_mod(xm_ref[...], n2, sc, sh).astype(_BF16)
        h_ref[tm:, :] = _halo_rows(i, pl.num_programs(0), xn_ref, xp_ref, n2, sc, sh).astype(_BF16)
        o_ref[...] = jnp.zeros_like(o_ref)

    hb = h_ref[...]

    def conv(w_ref, cw_ref):
        u = jnp.dot(hb, w_ref[...], preferred_element_type=_F32)
        up, uc, un = _shift_rows(u, tm)
        return up * cw_ref[0:1, :] + uc * cw_ref[1:2, :] + un * cw_ref[2:3, :]

    half_g = 0.5 * conv(wg_ref, cwg_ref)
    silu_g = half_g + half_g * jnp.tanh(half_g)
    act = silu_g * conv(wa_ref, cwa_ref)
    o_ref[...] += jnp.dot(act.astype(_BF16), wd_ref[...], preferred_element_type=_F32)

    @pl.when(j == pl.num_programs(1) - 1)
    def _():
        xs = xm_ref[...] + g_ref[0:1, :] * o_ref[...]
        o_ref[...] = _rms(xs, fn_ref[...])


def _conv_ffn(xs, norm2_w, mod, w_up_b, ffn_conv_w, w_down_b, final_w):
    n, d = xs.shape
    d_ff = w_down_b.shape[0]
    tm, tf = TM_FFN, TF_FFN
    assert n % tm == 0 and d_ff % tf == 0 and tm % HALO == 0
    nf = d_ff // tf
    r = tm // HALO
    last_halo = n // HALO - 1
    vec = pl.BlockSpec((1, d), lambda i, j: (0, 0))
    modspec = lambda k: pl.BlockSpec((SUBLANES_F32, d), lambda i, j: (0, k))
    return pl.pallas_call(
        _ffn_kernel,
        grid=(n // tm, nf),
        in_specs=[pl.BlockSpec((HALO, d), lambda i, j: (jnp.maximum(i * r - 1, 0), 0)),
                  pl.BlockSpec((tm, d), lambda i, j: (i, 0), pipeline_mode=pl.Buffered(1)),
                  pl.BlockSpec((HALO, d), lambda i, j: (jnp.minimum((i + 1) * r, last_halo), 0)),
                  vec, modspec(3), modspec(4), modspec(5),
                  pl.BlockSpec((d, tf), lambda i, j: (0, j)),
                  pl.BlockSpec((d, tf), lambda i, j: (0, j + nf)),
                  pl.BlockSpec((CONV_K, tf), lambda i, j: (0, j)),
                  pl.BlockSpec((CONV_K, tf), lambda i, j: (0, j + nf)),
                  pl.BlockSpec((tf, d), lambda i, j: (j, 0)),
                  vec],
        out_specs=pl.BlockSpec((tm, d), lambda i, j: (i, 0)),
        out_shape=jax.ShapeDtypeStruct((n, d), _F32),
        scratch_shapes=[pltpu.VMEM((tm + 2 * HALO, d), _BF16)],
        compiler_params=_params(2),
        name="conv_ffn",
    )(xs, xs, xs, norm2_w, mod, mod, mod, w_up_b, w_up_b, ffn_conv_w, ffn_conv_w, w_down_b, final_w)


def kernel(x, c, ctx, c_ctx, w_ada, b_ada, norm1_w, w_in, q_norm_w, k_norm_w, conv_w, attn_out_norm_w,
           conv_out_norm_w, w_o, norm2_w, w_ffn_up, ffn_conv_w, w_ffn_down, final_norm_w):
    b, n, d = x.shape
    assert b == 1 and w_ada.shape[0] == 1, "single batch element, single trunk layer"
    x2d = x[0]
    row = lambda v: v.reshape(1, -1)

    mod = _adaln(c, c_ctx, w_ada[0], b_ada[0])
    cos, sin = _rope_tables(max(n // GRID_W, GRID_W))

    w_in_b = w_in[0].astype(_BF16)
    qn_scaled = row(q_norm_w[0]) * (HEAD_DIM ** -0.5 * math.log2(math.e))
    kc, vc = _ctx_kv(ctx[0], row(norm1_w[0]), mod, w_in_b, row(k_norm_w[0]))
    q, k, v, conv_n = _in_proj(x2d, row(norm1_w[0]), mod, w_in_b, qn_scaled, row(k_norm_w[0]),
                               cos, sin, conv_w[0], row(conv_out_norm_w[0]))
    attn, (w_o_b, w_up_b, w_down_b) = _attention(
        q, kc, vc, k, v, [w_o, w_ffn_up, w_ffn_down])
    xs = _merge(attn, conv_n, row(attn_out_norm_w[0]), w_o_b, x2d, mod)
    out = _conv_ffn(xs, row(norm2_w[0]), mod, w_up_b, ffn_conv_w[0], w_down_b, row(final_norm_w))
    return out[None]
```

```python
import functools
import math

import jax
import jax.numpy as jnp
from jax import lax
from jax.experimental import pallas as pl
from jax.experimental.pallas import tpu as pltpu

GRID_W = 64
N_HEADS = 8
N_KV_HEADS = 2
HEAD_DIM = 128
GROUP = N_HEADS // N_KV_HEADS
ROPE_THETA = 10000.0
ROPE_AXIS_DIM = HEAD_DIM // 2
EPS = 1e-6
N_MOD = 6
CONV_K = 3

LANES = 128
SUBLANES_F32 = 8
BF16_ROWS = 16
VMEM_LIMIT_BYTES = 60 * 1024 * 1024

ADA_TN = 1024
TM_IN = 512
TQ = 256
TK = 2048
TM_MERGE = 512
TM_FFN = 1024
TF_FFN = 512
HALO = SUBLANES_F32

_BF16 = jnp.bfloat16
_F32 = jnp.float32


def _params(n_axes):
    return pltpu.CompilerParams(dimension_semantics=("arbitrary",) * n_axes,
                                vmem_limit_bytes=VMEM_LIMIT_BYTES)


def _rms(x, w):
    return x * lax.rsqrt(jnp.mean(x * x, axis=-1, keepdims=True) + EPS) * w


def _norm_mod(x, norm_w, scale, shift):
    w = norm_w * (1.0 + scale)
    return x * lax.rsqrt(jnp.mean(x * x, axis=-1, keepdims=True) + EPS) * w + shift


def _halo_rows(i, n_tiles, xn_ref, xp_ref, norm_w, scale, shift):
    xh = jnp.concatenate([xn_ref[...], xp_ref[...]], axis=0)
    r = lax.broadcasted_iota(jnp.int32, (2 * HALO, 1), 0)
    has_next = (i < n_tiles - 1).astype(jnp.int32)
    has_prev = (i > 0).astype(jnp.int32)
    valid = jnp.where(r < HALO, has_next, has_prev) > 0
    return jnp.where(valid, _norm_mod(xh, norm_w, scale, shift), 0.0)


def _shift_rows(u, tm):
    rows = u.shape[0]
    prev = pltpu.roll(u, 1, axis=0)[0:tm]
    nxt = pltpu.roll(u, rows - 1, axis=0)[0:tm]
    return prev, u[0:tm], nxt


def _adaln_kernel(cb_ref, w_ref, b_ref, o_ref):
    rows = []
    for r in range(2):
        cv = cb_ref[r]
        s = cv / (1.0 + jnp.exp(-cv))
        parts = []
        for j in range(ADA_TN // LANES):
            wj = w_ref[:, j * LANES:(j + 1) * LANES]
            parts.append(jnp.sum(s * wj, axis=0, keepdims=True))
        rows.append(jnp.concatenate(parts, axis=1) + b_ref[...])
    pad = jnp.zeros((SUBLANES_F32 - 2, ADA_TN), _F32)
    o_ref[...] = jnp.concatenate(rows + [pad], axis=0)


def _adaln(c, c_ctx, w_ada, b_ada):
    d = w_ada.shape[0]
    n_out = w_ada.shape[1]
    cvec = jnp.stack([c[0], c_ctx], axis=0)
    cb = jnp.broadcast_to(cvec[:, :, None], (2, d, LANES))
    return pl.pallas_call(
        _adaln_kernel,
        grid=(n_out // ADA_TN,),
        in_specs=[pl.BlockSpec((2, d, LANES), lambda j: (0, 0, 0)),
                  pl.BlockSpec((d, ADA_TN), lambda j: (0, j)),
                  pl.BlockSpec((1, ADA_TN), lambda j: (0, j))],
        out_specs=pl.BlockSpec((SUBLANES_F32, ADA_TN), lambda j: (0, j)),
        out_shape=jax.ShapeDtypeStruct((SUBLANES_F32, n_out), _F32),
        compiler_params=_params(1),
        name="adaln",
    )(cb, w_ada, b_ada.reshape(1, n_out))


def _rope_kernel(freq_ref, sign_ref, cos_ref, sin_ref):
    pos = lax.broadcasted_iota(jnp.int32, cos_ref.shape, 0)
    ang = pos.astype(_F32) * freq_ref[...]
    cos_ref[...] = jnp.cos(ang)
    sin_ref[...] = jnp.sin(ang) * sign_ref[...]


def _rope_tables(n_pos):
    half = ROPE_AXIS_DIM // 2
    freqs = ROPE_THETA ** (-jnp.arange(half, dtype=_F32) / half)
    freq = jnp.tile(freqs, HEAD_DIM // half).reshape(1, HEAD_DIM)
    sign = jnp.tile(jnp.concatenate([-jnp.ones((half,), _F32), jnp.ones((half,), _F32)]),
                    HEAD_DIM // ROPE_AXIS_DIM).reshape(1, HEAD_DIM)
    vec = pl.BlockSpec((1, HEAD_DIM), lambda i: (0, 0))
    tab = pl.BlockSpec((n_pos, HEAD_DIM), lambda i: (0, 0))
    return pl.pallas_call(
        _rope_kernel,
        grid=(1,),
        in_specs=[vec, vec],
        out_specs=[tab, tab],
        out_shape=[jax.ShapeDtypeStruct((n_pos, HEAD_DIM), _F32)] * 2,
        compiler_params=_params(1),
        name="rope_tables",
    )(freq, sign)


def _rope_tile(row_ref, col_ref):
    lane = lax.broadcasted_iota(jnp.int32, (GRID_W, HEAD_DIM), 1)
    col = col_ref[...]
    parts = [jnp.where(lane < ROPE_AXIS_DIM, jnp.broadcast_to(row_ref[g:g + 1, :], (GRID_W, HEAD_DIM)), col)
             for g in range(row_ref.shape[0])]
    return jnp.concatenate(parts, axis=0)


def _rope_apply(x, cos, sin_signed):
    half = ROPE_AXIS_DIM // 2
    lane = lax.broadcasted_iota(jnp.int32, x.shape, 1)
    first = (lane & (ROPE_AXIS_DIM - 1)) < half
    swapped = jnp.where(first, pltpu.roll(x, HEAD_DIM - half, axis=1), pltpu.roll(x, half, axis=1))
    return x * cos + swapped * sin_signed


def _ctx_kv_kernel(x_ref, n1_ref, sh_ref, sc_ref, w_ref, kn_ref, k_ref, v_ref):
    x = x_ref[...]
    h = _rms(x, n1_ref[...]) * (1.0 + sc_ref[1:2, :]) + sh_ref[1:2, :]
    p = jnp.dot(h.astype(_BF16), w_ref[...], preferred_element_type=_F32)
    kvw = N_KV_HEADS * HEAD_DIM
    for g in range(N_KV_HEADS):
        kg = p[:, g * HEAD_DIM:(g + 1) * HEAD_DIM]
        k_ref[:, g * HEAD_DIM:(g + 1) * HEAD_DIM] = _rms(kg, kn_ref[...]).astype(_BF16)
    v_ref[...] = p[:, kvw:].astype(_BF16)


def _ctx_kv(ctx2d, norm1_w, mod, w_in_b, k_norm_w):
    c_len, d = ctx2d.shape
    attn_w = N_HEADS * HEAD_DIM
    kvw = N_KV_HEADS * HEAD_DIM
    assert attn_w % (2 * kvw) == 0
    return pl.pallas_call(
        _ctx_kv_kernel,
        grid=(1,),
        in_specs=[pl.BlockSpec((c_len, d), lambda i: (0, 0)),
                  pl.BlockSpec((1, d), lambda i: (0, 0)),
                  pl.BlockSpec((SUBLANES_F32, d), lambda i: (0, 0)),
                  pl.BlockSpec((SUBLANES_F32, d), lambda i: (0, 1)),
                  pl.BlockSpec((d, 2 * kvw), lambda i: (0, attn_w // (2 * kvw))),
                  pl.BlockSpec((1, HEAD_DIM), lambda i: (0, 0))],
        out_specs=[pl.BlockSpec((c_len, kvw), lambda i: (0, 0))] * 2,
        out_shape=[jax.ShapeDtypeStruct((c_len, kvw), _BF16)] * 2,
        compiler_params=_params(1),
        name="ctx_kv",
    )(ctx2d, norm1_w, mod, mod, w_in_b, k_norm_w)


def _in_proj_kernel(xp_ref, xm_ref, xn_ref, n1_ref, sh_ref, sc_ref, w_ref, qn_ref, kn_ref,
                    cos_row_ref, cos_col_ref, sin_row_ref, sin_col_ref, cw_ref, con_ref,
                    q_ref, k_ref, v_ref, conv_ref):
    i = pl.program_id(0)
    tm = xm_ref.shape[0]
    attn_w = N_HEADS * HEAD_DIM
    kvw = N_KV_HEADS * HEAD_DIM
    conv_w = conv_ref.shape[1]

    n1, sc, sh = n1_ref[...], sc_ref[0:1, :], sh_ref[0:1, :]
    hb = _norm_mod(xm_ref[...], n1, sc, sh).astype(_BF16)
    hb_halo = _halo_rows(i, pl.num_programs(0), xn_ref, xp_ref, n1, sc, sh).astype(_BF16)
    hb_ext = jnp.concatenate([hb, hb_halo], axis=0)
    cos = _rope_tile(cos_row_ref, cos_col_ref)
    sin = _rope_tile(sin_row_ref, sin_col_ref)

    def proj(lhs, lo, width):
        return jnp.dot(lhs, w_ref[:, lo:lo + width], preferred_element_type=_F32)

    step = 2 * HEAD_DIM
    for c0 in range(0, attn_w, step):
        pq = proj(hb, c0, step)
        for s in range(0, step, HEAD_DIM):
            qh = _rms(pq[:, s:s + HEAD_DIM], qn_ref[...])
            q_ref[:, c0 + s:c0 + s + HEAD_DIM] = _rope_apply(qh, cos, sin).astype(_BF16)
    pk = proj(hb, attn_w, kvw)
    for s in range(0, kvw, HEAD_DIM):
        kh = _rms(pk[:, s:s + HEAD_DIM], kn_ref[...])
        k_ref[:, s:s + HEAD_DIM] = _rope_apply(kh, cos, sin).astype(_BF16)
    v_ref[...] = proj(hb, attn_w + kvw, kvw).astype(_BF16)

    o = attn_w + 2 * kvw
    u = proj(hb_ext, o + conv_w, conv_w) * proj(hb_ext, o + 2 * conv_w, conv_w)
    up, uc, un = _shift_rows(u, tm)
    conv = up * cw_ref[0:1, :] + uc * cw_ref[1:2, :] + un * cw_ref[2:3, :]
    y = proj(hb, o, conv_w) * conv
    conv_ref[...] = _rms(y, con_ref[...]).astype(_BF16)


def _in_proj(x2d, norm1_w, mod, w_in_b, qn_scaled, k_norm_w, cos, sin, conv_w, con_w):
    n, d = x2d.shape
    tm = TM_IN
    assert n % tm == 0 and tm % HALO == 0
    assert tm == SUBLANES_F32 * GRID_W and cos.shape[0] >= max(n // GRID_W, GRID_W)
    rope_row = pl.BlockSpec((SUBLANES_F32, HEAD_DIM), lambda i: (i, 0))
    rope_col = pl.BlockSpec((GRID_W, HEAD_DIM), lambda i: (0, 0))
    attn_w = N_HEADS * HEAD_DIM
    kvw = N_KV_HEADS * HEAD_DIM
    cw = conv_w.shape[1]
    r = tm // HALO
    last_halo = n // HALO - 1
    vec = lambda width: pl.BlockSpec((1, width), lambda i: (0, 0))
    row = lambda width: pl.BlockSpec((tm, width), lambda i: (i, 0))
    return pl.pallas_call(
        _in_proj_kernel,
        grid=(n // tm,),
        in_specs=[pl.BlockSpec((HALO, d), lambda i: (jnp.maximum(i * r - 1, 0), 0)),
                  row(d),
                  pl.BlockSpec((HALO, d), lambda i: (jnp.minimum((i + 1) * r, last_halo), 0)),
                  vec(d),
                  pl.BlockSpec((SUBLANES_F32, d), lambda i: (0, 0)),
                  pl.BlockSpec((SUBLANES_F32, d), lambda i: (0, 1)),
                  pl.BlockSpec(w_in_b.shape, lambda i: (0, 0), pipeline_mode=pl.Buffered(1)),
                  vec(HEAD_DIM), vec(HEAD_DIM),
                  rope_row, rope_col, rope_row, rope_col,
                  pl.BlockSpec((CONV_K, cw), lambda i: (0, 0)),
                  vec(cw)],
        out_specs=[row(attn_w), row(kvw), row(kvw), row(cw)],
        out_shape=[jax.ShapeDtypeStruct((n, attn_w), _BF16),
                   jax.ShapeDtypeStruct((n, kvw), _BF16),
                   jax.ShapeDtypeStruct((n, kvw), _BF16),
                   jax.ShapeDtypeStruct((n, cw), _BF16)],
        compiler_params=_params(1),
        name="in_proj",
    )(x2d, x2d, x2d, norm1_w, mod, mod, w_in_b, qn_scaled, k_norm_w, cos, cos, sin, sin, conv_w, con_w)


def _attn_kernel(q_ref, kc_ref, vc_ref, k_ref, v_ref, w1_ref, w2_ref, w3_ref,
                 o_ref, w1b_ref, w2b_ref, w3b_ref, m_ref, acc_ref, sa_ref, sb_ref):
    for src, dst in ((w1_ref, w1b_ref), (w2_ref, w2b_ref), (w3_ref, w3b_ref)):
        dst[...] = src[...].astype(_BF16)
    tq = q_ref.shape[0]
    nk = k_ref.shape[0] // TK
    qg = jnp.concatenate([q_ref[:, j * HEAD_DIM:(j + 1) * HEAD_DIM] for j in range(GROUP)], axis=0)

    def scores(kb):
        return lax.dot_general(qg, kb, (((1,), (1,)), ((), ())), preferred_element_type=_F32)


    def process(s_ref, vb):
        tk = vb.shape[0]
        sc = [s_ref[:, c * LANES:(c + 1) * LANES] for c in range(tk // LANES)]
        mc = functools.reduce(jnp.maximum, sc)
        m_prev = m_ref[...]
        m_new = jnp.maximum(m_prev, jnp.max(mc, axis=1, keepdims=True))
        alpha = jnp.exp2(m_prev - m_new)
        p = jnp.concatenate([jnp.exp2(x - m_new).astype(_BF16) for x in sc], axis=1)
        vb_ones = jnp.concatenate([vb, jnp.ones((tk, HEAD_DIM), _BF16)], axis=1)
        pv = jnp.dot(p, vb_ones, preferred_element_type=_F32)
        acc_ref[...] = jnp.concatenate([alpha, alpha], axis=1) * acc_ref[...] + pv
        m_ref[...] = m_new

    m_ref[...] = jnp.full(m_ref.shape, -jnp.inf, _F32)
    acc_ref[...] = jnp.zeros(acc_ref.shape, _F32)
    blocks = [(k_ref, v_ref, b * TK, TK) for b in range(nk)] + [(kc_ref, vc_ref, 0, kc_ref.shape[0])]
    bufs = (sa_ref, sb_ref)

    def put_scores(i):
        kr, _, lo, size = blocks[i]
        bufs[i % 2][:, 0:size] = scores(kr[lo:lo + size, :])

    put_scores(0)
    for i, (_, vr, lo, size) in enumerate(blocks):
        if i + 1 < len(blocks):
            put_scores(i + 1)
        process(bufs[i % 2], vr[lo:lo + size, :])

    og = acc_ref[:, 0:HEAD_DIM] / acc_ref[:, HEAD_DIM:2 * HEAD_DIM]
    for j in range(GROUP):
        o_ref[:, j * HEAD_DIM:(j + 1) * HEAD_DIM] = og[j * tq:(j + 1) * tq].astype(_BF16)


def _attention(q, kc, vc, k, v, f32_weights):
    n, attn_w = q.shape
    c_len = kc.shape[0]
    gw = GROUP * HEAD_DIM
    assert n % TQ == 0 and n % TK == 0 and c_len % LANES == 0 and c_len <= TK
    rows = GROUP * TQ
    nq = n // TQ
    steps = N_KV_HEADS * nq
    kv = lambda a: pl.BlockSpec((a.shape[0], HEAD_DIM), lambda g, i: (0, g), pipeline_mode=pl.Buffered(1))

    in_slabs, out_slabs, out_shapes = [], [], []
    for w in f32_weights:
        _, r, c = w.shape
        rep = next(m for m in range(1, steps + 1)
                   if steps % m == 0 and (r * m) % (steps * BF16_ROWS) == 0)
        sr = r * rep // steps
        in_slabs.append(pl.BlockSpec((None, sr, c), lambda g, i, rep=rep: (0, (g * nq + i) // rep, 0)))
        out_slabs.append(pl.BlockSpec((sr, c), lambda g, i, rep=rep: ((g * nq + i) // rep, 0)))
        out_shapes.append(jax.ShapeDtypeStruct((r, c), _BF16))

    outs = pl.pallas_call(
        _attn_kernel,
        grid=(N_KV_HEADS, nq),
        in_specs=[pl.BlockSpec((TQ, gw), lambda g, i: (i, g)),
                  kv(kc), kv(vc), kv(k), kv(v)] + in_slabs,
        out_specs=[pl.BlockSpec((TQ, gw), lambda g, i: (i, g))] + out_slabs,
        out_shape=[jax.ShapeDtypeStruct((n, attn_w), _BF16)] + out_shapes,
        scratch_shapes=[pltpu.VMEM((rows, LANES), _F32),
                        pltpu.VMEM((rows, 2 * HEAD_DIM), _F32),
                        pltpu.VMEM((rows, TK), _F32),
                        pltpu.VMEM((rows, TK), _F32)],
        compiler_params=_params(2),
        name="attention",
    )(q, kc, vc, k, v, *f32_weights)
    return outs[0], outs[1:]


def _merge_kernel(a_ref, c_ref, aon_ref, w_ref, x_ref, g_ref, o_ref):
    aw = a_ref.shape[1]
    an = _rms(a_ref[...].astype(_F32), aon_ref[...]).astype(_BF16)
    y = jnp.dot(an, w_ref[0:aw, :], preferred_element_type=_F32)
    y = y + jnp.dot(c_ref[...], w_ref[aw:, :], preferred_element_type=_F32)
    o_ref[...] = x_ref[...] + g_ref[0:1, :] * y


def _merge(attn, conv_n, aon_w, w_o_b, x2d, mod):
    n, d = x2d.shape
    tm = TM_MERGE
    assert n % tm == 0
    row = lambda width: pl.BlockSpec((tm, width), lambda i: (i, 0))
    return pl.pallas_call(
        _merge_kernel,
        grid=(n // tm,),
        in_specs=[row(attn.shape[1]), row(conv_n.shape[1]),
                  pl.BlockSpec((1, attn.shape[1]), lambda i: (0, 0)),
                  pl.BlockSpec(w_o_b.shape, lambda i: (0, 0), pipeline_mode=pl.Buffered(1)),
                  row(d),
                  pl.BlockSpec((SUBLANES_F32, d), lambda i: (0, 2))],
        out_specs=row(d),
        out_shape=jax.ShapeDtypeStruct((n, d), _F32),
        compiler_params=_params(1),
        name="merge",
    )(attn, conv_n, aon_w, w_o_b, x2d, mod)


def _ffn_kernel(xp_ref, x_hbm_ref, xn_ref, n2_ref, sh_ref, sc_ref, g_ref, wa_ref, wg_ref,
                cwa_ref, cwg_ref, wd_ref, fn_ref, o_ref, h_ref, xm_ref, x_sem):
    i = pl.program_id(0)
    j = pl.program_id(1)
    n_tiles = pl.num_programs(0)
    tm = xm_ref.shape[0]

    def x_copy(tile):
        return pltpu.make_async_copy(x_hbm_ref.at[pl.ds(tile * tm, tm), :], xm_ref, x_sem)

    @pl.when((j == 0) & (i == 0))
    def _():
        x_copy(0).start()

    @pl.when(j == 0)
    def _():
        x_copy(i).wait()
        n2, sc, sh = n2_ref[...], sc_ref[0:1, :], sh_ref[0:1, :]
        h_ref[0:tm, :] = _norm_mod(xm_ref[...], n2, sc, sh).astype(_BF16)
        h_ref[tm:, :] = _halo_rows(i, n_tiles, xn_ref, xp_ref, n2, sc, sh).astype(_BF16)
        o_ref[...] = xm_ref[...]

    @pl.when((j == 1) & (i + 1 < n_tiles))
    def _():
        x_copy(i + 1).start()

    hb = h_ref[...]

    def conv(w_ref, cw_ref):
        u = jnp.dot(hb, w_ref[...], preferred_element_type=_F32)
        up, uc, un = _shift_rows(u, tm)
        return up * cw_ref[0:1, :] + uc * cw_ref[1:2, :] + un * cw_ref[2:3, :]

    half_g = 0.5 * conv(wg_ref, cwg_ref)
    silu_g = half_g + half_g * jnp.tanh(half_g)
    act = silu_g * conv(wa_ref, cwa_ref)
    o_ref[...] += g_ref[0:1, :] * jnp.dot(act.astype(_BF16), wd_ref[...], preferred_element_type=_F32)

    @pl.when(j == pl.num_programs(1) - 1)
    def _():
        o_ref[...] = _rms(o_ref[...], fn_ref[...])


def _conv_ffn(xs, norm2_w, mod, w_up_b, ffn_conv_w, w_down_b, final_w):
    n, d = xs.shape
    d_ff = w_down_b.shape[0]
    tm, tf = TM_FFN, TF_FFN
    assert n % tm == 0 and d_ff % tf == 0 and tm % HALO == 0
    nf = d_ff // tf
    assert nf >= 2, "the next tile's rows are prefetched at j == 1"
    r = tm // HALO
    last_halo = n // HALO - 1
    vec = pl.BlockSpec((1, d), lambda i, j: (0, 0))
    modspec = lambda k: pl.BlockSpec((SUBLANES_F32, d), lambda i, j: (0, k))
    return pl.pallas_call(
        _ffn_kernel,
        grid=(n // tm, nf),
        in_specs=[pl.BlockSpec((HALO, d), lambda i, j: (jnp.maximum(i * r - 1, 0), 0)),
                  pl.BlockSpec(memory_space=pl.ANY),
                  pl.BlockSpec((HALO, d), lambda i, j: (jnp.minimum((i + 1) * r, last_halo), 0)),
                  vec, modspec(3), modspec(4), modspec(5),
                  pl.BlockSpec((d, tf), lambda i, j: (0, j)),
                  pl.BlockSpec((d, tf), lambda i, j: (0, j + nf)),
                  pl.BlockSpec((CONV_K, tf), lambda i, j: (0, j)),
                  pl.BlockSpec((CONV_K, tf), lambda i, j: (0, j + nf)),
                  pl.BlockSpec((tf, d), lambda i, j: (j, 0)),
                  vec],
        out_specs=pl.BlockSpec((tm, d), lambda i, j: (i, 0)),
        out_shape=jax.ShapeDtypeStruct((n, d), _F32),
        scratch_shapes=[pltpu.VMEM((tm + 2 * HALO, d), _BF16),
                        pltpu.VMEM((tm, d), _F32),
                        pltpu.SemaphoreType.DMA(())],
        compiler_params=_params(2),
        name="conv_ffn",
    )(xs, xs, xs, norm2_w, mod, mod, mod, w_up_b, w_up_b, ffn_conv_w, ffn_conv_w, w_down_b, final_w)


def kernel(x, c, ctx, c_ctx, w_ada, b_ada, norm1_w, w_in, q_norm_w, k_norm_w, conv_w, attn_out_norm_w,
           conv_out_norm_w, w_o, norm2_w, w_ffn_up, ffn_conv_w, w_ffn_down, final_norm_w):
    b, n, d = x.shape
    assert b == 1 and w_ada.shape[0] == 1, "single batch element, single trunk layer"
    x2d = x[0]
    row = lambda v: v.reshape(1, -1)

    mod = _adaln(c, c_ctx, w_ada[0], b_ada[0])
    cos, sin = _rope_tables(max(n // GRID_W, GRID_W))

    w_in_b = w_in[0].astype(_BF16)
    qn_scaled = row(q_norm_w[0]) * (HEAD_DIM ** -0.5 * math.log2(math.e))
    kc, vc = _ctx_kv(ctx[0], row(norm1_w[0]), mod, w_in_b, row(k_norm_w[0]))
    q, k, v, conv_n = _in_proj(x2d, row(norm1_w[0]), mod, w_in_b, qn_scaled, row(k_norm_w[0]),
                               cos, sin, conv_w[0], row(conv_out_norm_w[0]))
    attn, (w_o_b, w_up_b, w_down_b) = _attention(
        q, kc, vc, k, v, [w_o, w_ffn_up, w_ffn_down])
    xs = _merge(attn, conv_n, row(attn_out_norm_w[0]), w_o_b, x2d, mod)
    out = _conv_ffn(xs, row(norm2_w[0]), mod, w_up_b, ffn_conv_w[0], w_down_b, row(final_norm_w))
    return out[None]
```

```python
import functools
import math

import jax
import jax.numpy as jnp
from jax import lax
from jax.experimental import pallas as pl
from jax.experimental.pallas import tpu as pltpu

GRID_W = 64
N_HEADS = 8
N_KV_HEADS = 2
HEAD_DIM = 128
GROUP = N_HEADS // N_KV_HEADS
ROPE_THETA = 10000.0
ROPE_AXIS_DIM = HEAD_DIM // 2
EPS = 1e-6
N_MOD = 6
CONV_K = 3

LANES = 128
SUBLANES_F32 = 8
BF16_ROWS = 16
VMEM_LIMIT_BYTES = 60 * 1024 * 1024

ADA_TN = 1024
TM_IN = 512
TQ = 256
TK = 2048
TM_MERGE = 512
TM_FFN = 1024
TF_FFN = 512
HALO = SUBLANES_F32

_BF16 = jnp.bfloat16
_F32 = jnp.float32


def _params(n_axes):
    return pltpu.CompilerParams(dimension_semantics=("arbitrary",) * n_axes,
                                vmem_limit_bytes=VMEM_LIMIT_BYTES)


def _rms(x, w):
    return x * lax.rsqrt(jnp.mean(x * x, axis=-1, keepdims=True) + EPS) * w


def _norm_mod(x, norm_w, scale, shift):
    w = norm_w * (1.0 + scale)
    return x * lax.rsqrt(jnp.mean(x * x, axis=-1, keepdims=True) + EPS) * w + shift


def _halo_rows(i, n_tiles, xn_ref, xp_ref, norm_w, scale, shift):
    xh = jnp.concatenate([xn_ref[...], xp_ref[...]], axis=0)
    r = lax.broadcasted_iota(jnp.int32, (2 * HALO, 1), 0)
    has_next = (i < n_tiles - 1).astype(jnp.int32)
    has_prev = (i > 0).astype(jnp.int32)
    valid = jnp.where(r < HALO, has_next, has_prev) > 0
    return jnp.where(valid, _norm_mod(xh, norm_w, scale, shift), 0.0)


def _shift_rows(u, tm):
    rows = u.shape[0]
    prev = pltpu.roll(u, 1, axis=0)[0:tm]
    nxt = pltpu.roll(u, rows - 1, axis=0)[0:tm]
    return prev, u[0:tm], nxt


def _adaln_kernel(cb_ref, w_ref, b_ref, o_ref):
    rows = []
    for r in range(2):
        cv = cb_ref[r]
        s = cv / (1.0 + jnp.exp(-cv))
        parts = []
        for j in range(ADA_TN // LANES):
            wj = w_ref[:, j * LANES:(j + 1) * LANES]
            parts.append(jnp.sum(s * wj, axis=0, keepdims=True))
        rows.append(jnp.concatenate(parts, axis=1) + b_ref[...])
    pad = jnp.zeros((SUBLANES_F32 - 2, ADA_TN), _F32)
    o_ref[...] = jnp.concatenate(rows + [pad], axis=0)


def _adaln(c, c_ctx, w_ada, b_ada):
    d = w_ada.shape[0]
    n_out = w_ada.shape[1]
    cvec = jnp.stack([c[0], c_ctx], axis=0)
    cb = jnp.broadcast_to(cvec[:, :, None], (2, d, LANES))
    return pl.pallas_call(
        _adaln_kernel,
        grid=(n_out // ADA_TN,),
        in_specs=[pl.BlockSpec((2, d, LANES), lambda j: (0, 0, 0)),
                  pl.BlockSpec((d, ADA_TN), lambda j: (0, j)),
                  pl.BlockSpec((1, ADA_TN), lambda j: (0, j))],
        out_specs=pl.BlockSpec((SUBLANES_F32, ADA_TN), lambda j: (0, j)),
        out_shape=jax.ShapeDtypeStruct((SUBLANES_F32, n_out), _F32),
        compiler_params=_params(1),
        name="adaln",
    )(cb, w_ada, b_ada.reshape(1, n_out))


def _rope_kernel(freq_ref, sign_ref, cos_ref, sin_ref):
    pos = lax.broadcasted_iota(jnp.int32, cos_ref.shape, 0)
    ang = pos.astype(_F32) * freq_ref[...]
    cos_ref[...] = jnp.cos(ang)
    sin_ref[...] = jnp.sin(ang) * sign_ref[...]


def _rope_tables(n_pos):
    half = ROPE_AXIS_DIM // 2
    freqs = ROPE_THETA ** (-jnp.arange(half, dtype=_F32) / half)
    freq = jnp.tile(freqs, HEAD_DIM // half).reshape(1, HEAD_DIM)
    sign = jnp.tile(jnp.concatenate([-jnp.ones((half,), _F32), jnp.ones((half,), _F32)]),
                    HEAD_DIM // ROPE_AXIS_DIM).reshape(1, HEAD_DIM)
    vec = pl.BlockSpec((1, HEAD_DIM), lambda i: (0, 0))
    tab = pl.BlockSpec((n_pos, HEAD_DIM), lambda i: (0, 0))
    return pl.pallas_call(
        _rope_kernel,
        grid=(1,),
        in_specs=[vec, vec],
        out_specs=[tab, tab],
        out_shape=[jax.ShapeDtypeStruct((n_pos, HEAD_DIM), _F32)] * 2,
        compiler_params=_params(1),
        name="rope_tables",
    )(freq, sign)


def _rope_tile(row_ref, col_ref):
    lane = lax.broadcasted_iota(jnp.int32, (GRID_W, HEAD_DIM), 1)
    col = col_ref[...]
    parts = [jnp.where(lane < ROPE_AXIS_DIM, jnp.broadcast_to(row_ref[g:g + 1, :], (GRID_W, HEAD_DIM)), col)
             for g in range(row_ref.shape[0])]
    return jnp.concatenate(parts, axis=0)


def _rope_apply(x, cos, sin_signed):
    half = ROPE_AXIS_DIM // 2
    lane = lax.broadcasted_iota(jnp.int32, x.shape, 1)
    first = (lane & (ROPE_AXIS_DIM - 1)) < half
    swapped = jnp.where(first, pltpu.roll(x, HEAD_DIM - half, axis=1), pltpu.roll(x, half, axis=1))
    return x * cos + swapped * sin_signed


def _ctx_kv_kernel(x_ref, n1_ref, sh_ref, sc_ref, w_ref, kn_ref, k_ref, v_ref):
    x = x_ref[...]
    h = _rms(x, n1_ref[...]) * (1.0 + sc_ref[1:2, :]) + sh_ref[1:2, :]
    p = jnp.dot(h.astype(_BF16), w_ref[...], preferred_element_type=_F32)
    kvw = N_KV_HEADS * HEAD_DIM
    for g in range(N_KV_HEADS):
        kg = p[:, g * HEAD_DIM:(g + 1) * HEAD_DIM]
        k_ref[:, g * HEAD_DIM:(g + 1) * HEAD_DIM] = _rms(kg, kn_ref[...]).astype(_BF16)
    v_ref[...] = p[:, kvw:].astype(_BF16)


def _ctx_kv(ctx2d, norm1_w, mod, w_in_b, k_norm_w):
    c_len, d = ctx2d.shape
    attn_w = N_HEADS * HEAD_DIM
    kvw = N_KV_HEADS * HEAD_DIM
    assert attn_w % (2 * kvw) == 0
    return pl.pallas_call(
        _ctx_kv_kernel,
        grid=(1,),
        in_specs=[pl.BlockSpec((c_len, d), lambda i: (0, 0)),
                  pl.BlockSpec((1, d), lambda i: (0, 0)),
                  pl.BlockSpec((SUBLANES_F32, d), lambda i: (0, 0)),
                  pl.BlockSpec((SUBLANES_F32, d), lambda i: (0, 1)),
                  pl.BlockSpec((d, 2 * kvw), lambda i: (0, attn_w // (2 * kvw))),
                  pl.BlockSpec((1, HEAD_DIM), lambda i: (0, 0))],
        out_specs=[pl.BlockSpec((c_len, kvw), lambda i: (0, 0))] * 2,
        out_shape=[jax.ShapeDtypeStruct((c_len, kvw), _BF16)] * 2,
        compiler_params=_params(1),
        name="ctx_kv",
    )(ctx2d, norm1_w, mod, mod, w_in_b, k_norm_w)


def _in_proj_kernel(xp_ref, xm_ref, xn_ref, n1_ref, sh_ref, sc_ref, w_ref, qn_ref, kn_ref,
                    cos_row_ref, cos_col_ref, sin_row_ref, sin_col_ref, cw_ref, con_ref,
                    q_ref, k_ref, v_ref, conv_ref):
    i = pl.program_id(0)
    tm = xm_ref.shape[0]
    attn_w = N_HEADS * HEAD_DIM
    kvw = N_KV_HEADS * HEAD_DIM
    conv_w = conv_ref.shape[1]

    n1, sc, sh = n1_ref[...], sc_ref[0:1, :], sh_ref[0:1, :]
    hb = _norm_mod(xm_ref[...], n1, sc, sh).astype(_BF16)
    hb_halo = _halo_rows(i, pl.num_programs(0), xn_ref, xp_ref, n1, sc, sh).astype(_BF16)
    hb_ext = jnp.concatenate([hb, hb_halo], axis=0)
    cos = _rope_tile(cos_row_ref, cos_col_ref)
    sin = _rope_tile(sin_row_ref, sin_col_ref)

    def proj(lhs, lo, width):
        return jnp.dot(lhs, w_ref[:, lo:lo + width], preferred_element_type=_F32)

    step = 2 * HEAD_DIM
    for c0 in range(0, attn_w, step):
        pq = proj(hb, c0, step)
        for s in range(0, step, HEAD_DIM):
            qh = _rms(pq[:, s:s + HEAD_DIM], qn_ref[...])
            q_ref[:, c0 + s:c0 + s + HEAD_DIM] = _rope_apply(qh, cos, sin).astype(_BF16)
    pk = proj(hb, attn_w, kvw)
    for s in range(0, kvw, HEAD_DIM):
        kh = _rms(pk[:, s:s + HEAD_DIM], kn_ref[...])
        k_ref[:, s:s + HEAD_DIM] = _rope_apply(kh, cos, sin).astype(_BF16)
    v_ref[...] = proj(hb, attn_w + kvw, kvw).astype(_BF16)

    o = attn_w + 2 * kvw
    u = proj(hb_ext, o + conv_w, conv_w) * proj(hb_ext, o + 2 * conv_w, conv_w)
    up, uc, un = _shift_rows(u, tm)
    conv = up * cw_ref[0:1, :] + uc * cw_ref[1:2, :] + un * cw_ref[2:3, :]
    y = proj(hb, o, conv_w) * conv
    conv_ref[...] = _rms(y, con_ref[...]).astype(_BF16)


def _in_proj(x2d, norm1_w, mod, w_in_b, qn_scaled, k_norm_w, cos, sin, conv_w, con_w):
    n, d = x2d.shape
    tm = TM_IN
    assert n % tm == 0 and tm % HALO == 0
    assert tm == SUBLANES_F32 * GRID_W and cos.shape[0] >= max(n // GRID_W, GRID_W)
    rope_row = pl.BlockSpec((SUBLANES_F32, HEAD_DIM), lambda i: (i, 0))
    rope_col = pl.BlockSpec((GRID_W, HEAD_DIM), lambda i: (0, 0))
    attn_w = N_HEADS * HEAD_DIM
    kvw = N_KV_HEADS * HEAD_DIM
    cw = conv_w.shape[1]
    r = tm // HALO
    last_halo = n // HALO - 1
    vec = lambda width: pl.BlockSpec((1, width), lambda i: (0, 0))
    row = lambda width: pl.BlockSpec((tm, width), lambda i: (i, 0))
    return pl.pallas_call(
        _in_proj_kernel,
        grid=(n // tm,),
        in_specs=[pl.BlockSpec((HALO, d), lambda i: (jnp.maximum(i * r - 1, 0), 0)),
                  row(d),
                  pl.BlockSpec((HALO, d), lambda i: (jnp.minimum((i + 1) * r, last_halo), 0)),
                  vec(d),
                  pl.BlockSpec((SUBLANES_F32, d), lambda i: (0, 0)),
                  pl.BlockSpec((SUBLANES_F32, d), lambda i: (0, 1)),
                  pl.BlockSpec(w_in_b.shape, lambda i: (0, 0), pipeline_mode=pl.Buffered(1)),
                  vec(HEAD_DIM), vec(HEAD_DIM),
                  rope_row, rope_col, rope_row, rope_col,
                  pl.BlockSpec((CONV_K, cw), lambda i: (0, 0)),
                  vec(cw)],
        out_specs=[row(attn_w), row(kvw), row(kvw), row(cw)],
        out_shape=[jax.ShapeDtypeStruct((n, attn_w), _BF16),
                   jax.ShapeDtypeStruct((n, kvw), _BF16),
                   jax.ShapeDtypeStruct((n, kvw), _BF16),
                   jax.ShapeDtypeStruct((n, cw), _BF16)],
        compiler_params=_params(1),
        name="in_proj",
    )(x2d, x2d, x2d, norm1_w, mod, mod, w_in_b, qn_scaled, k_norm_w, cos, cos, sin, sin, conv_w, con_w)


def _attn_kernel(q_ref, kc_ref, vc_ref, k_ref, v_ref, w1_ref, w2_ref, w3_ref,
                 o_ref, w1b_ref, w2b_ref, w3b_ref, m_ref, acc_ref, sa_ref, sb_ref):
    for src, dst in ((w1_ref, w1b_ref), (w2_ref, w2b_ref), (w3_ref, w3b_ref)):
        dst[...] = src[...].astype(_BF16)
    tq = q_ref.shape[0]
    nk = k_ref.shape[0] // TK
    qg = jnp.concatenate([q_ref[:, j * HEAD_DIM:(j + 1) * HEAD_DIM] for j in range(GROUP)], axis=0)

    def scores(kb):
        return lax.dot_general(qg, kb, (((1,), (1,)), ((), ())), preferred_element_type=_F32)


    def process(s_ref, vb):
        tk = vb.shape[0]
        sc = [s_ref[:, c * LANES:(c + 1) * LANES] for c in range(tk // LANES)]
        mc = functools.reduce(jnp.maximum, sc)
        m_prev = m_ref[...]
        m_new = jnp.maximum(m_prev, jnp.max(mc, axis=1, keepdims=True))
        alpha = jnp.exp2(m_prev - m_new)
        p = jnp.concatenate([jnp.exp2(x - m_new).astype(_BF16) for x in sc], axis=1)
        vb_ones = jnp.concatenate([vb, jnp.ones((tk, HEAD_DIM), _BF16)], axis=1)
        pv = jnp.dot(p, vb_ones, preferred_element_type=_F32)
        acc_ref[...] = jnp.concatenate([alpha, alpha], axis=1) * acc_ref[...] + pv
        m_ref[...] = m_new

    m_ref[...] = jnp.full(m_ref.shape, -jnp.inf, _F32)
    acc_ref[...] = jnp.zeros(acc_ref.shape, _F32)
    blocks = [(k_ref, v_ref, b * TK, TK) for b in range(nk)] + [(kc_ref, vc_ref, 0, kc_ref.shape[0])]
    bufs = (sa_ref, sb_ref)

    def put_scores(i):
        kr, _, lo, size = blocks[i]
        bufs[i % 2][:, 0:size] = scores(kr[lo:lo + size, :])

    put_scores(0)
    for i, (_, vr, lo, size) in enumerate(blocks):
        if i + 1 < len(blocks):
            put_scores(i + 1)
        process(bufs[i % 2], vr[lo:lo + size, :])

    og = acc_ref[:, 0:HEAD_DIM] / acc_ref[:, HEAD_DIM:2 * HEAD_DIM]
    for j in range(GROUP):
        o_ref[:, j * HEAD_DIM:(j + 1) * HEAD_DIM] = og[j * tq:(j + 1) * tq].astype(_BF16)


def _attention(q, kc, vc, k, v, f32_weights):
    n, attn_w = q.shape
    c_len = kc.shape[0]
    gw = GROUP * HEAD_DIM
    assert n % TQ == 0 and n % TK == 0 and c_len % LANES == 0 and c_len <= TK
    rows = GROUP * TQ
    nq = n // TQ
    steps = N_KV_HEADS * nq
    kv = lambda a: pl.BlockSpec((a.shape[0], HEAD_DIM), lambda g, i: (0, g))

    in_slabs, out_slabs, out_shapes = [], [], []
    for w in f32_weights:
        _, r, c = w.shape
        rep = next(m for m in range(1, steps + 1)
                   if steps % m == 0 and (r * m) % (steps * BF16_ROWS) == 0)
        sr = r * rep // steps
        in_slabs.append(pl.BlockSpec((None, sr, c), lambda g, i, rep=rep: (0, (g * nq + i) // rep, 0)))
        out_slabs.append(pl.BlockSpec((sr, c), lambda g, i, rep=rep: ((g * nq + i) // rep, 0)))
        out_shapes.append(jax.ShapeDtypeStruct((r, c), _BF16))

    outs = pl.pallas_call(
        _attn_kernel,
        grid=(N_KV_HEADS, nq),
        in_specs=[pl.BlockSpec((TQ, gw), lambda g, i: (i, g)),
                  kv(kc), kv(vc), kv(k), kv(v)] + in_slabs,
        out_specs=[pl.BlockSpec((TQ, gw), lambda g, i: (i, g))] + out_slabs,
        out_shape=[jax.ShapeDtypeStruct((n, attn_w), _BF16)] + out_shapes,
        scratch_shapes=[pltpu.VMEM((rows, LANES), _F32),
                        pltpu.VMEM((rows, 2 * HEAD_DIM), _F32),
                        pltpu.VMEM((rows, TK), _F32),
                        pltpu.VMEM((rows, TK), _F32)],
        compiler_params=_params(2),
        name="attention",
    )(q, kc, vc, k, v, *f32_weights)
    return outs[0], outs[1:]


def _merge_kernel(a_ref, c_ref, aon_ref, w_ref, x_ref, g_ref, o_ref):
    aw = a_ref.shape[1]
    an = _rms(a_ref[...].astype(_F32), aon_ref[...]).astype(_BF16)
    y = jnp.dot(an, w_ref[0:aw, :], preferred_element_type=_F32)
    y = y + jnp.dot(c_ref[...], w_ref[aw:, :], preferred_element_type=_F32)
    o_ref[...] = x_ref[...] + g_ref[0:1, :] * y


def _merge(attn, conv_n, aon_w, w_o_b, x2d, mod):
    n, d = x2d.shape
    tm = TM_MERGE
    assert n % tm == 0
    row = lambda width: pl.BlockSpec((tm, width), lambda i: (i, 0))
    return pl.pallas_call(
        _merge_kernel,
        grid=(n // tm,),
        in_specs=[row(attn.shape[1]), row(conv_n.shape[1]),
                  pl.BlockSpec((1, attn.shape[1]), lambda i: (0, 0)),
                  pl.BlockSpec(w_o_b.shape, lambda i: (0, 0), pipeline_mode=pl.Buffered(1)),
                  row(d),
                  pl.BlockSpec((SUBLANES_F32, d), lambda i: (0, 2))],
        out_specs=row(d),
        out_shape=jax.ShapeDtypeStruct((n, d), _F32),
        compiler_params=_params(1),
        name="merge",
    )(attn, conv_n, aon_w, w_o_b, x2d, mod)


def _ffn_kernel(xp_ref, x_hbm_ref, xn_ref, n2_ref, sh_ref, sc_ref, g_ref, wa_ref, wg_ref,
                cwa_ref, cwg_ref, wd_ref, fn_ref, o_ref, h_ref, xm_ref, x_sem):
    i = pl.program_id(0)
    j = pl.program_id(1)
    n_tiles = pl.num_programs(0)
    tm = xm_ref.shape[0]

    def x_copy(tile):
        return pltpu.make_async_copy(x_hbm_ref.at[pl.ds(tile * tm, tm), :], xm_ref, x_sem)

    @pl.when((j == 0) & (i == 0))
    def _():
        x_copy(0).start()

    @pl.when(j == 0)
    def _():
        x_copy(i).wait()
        n2, sc, sh = n2_ref[...], sc_ref[0:1, :], sh_ref[0:1, :]
        h_ref[0:tm, :] = _norm_mod(xm_ref[...], n2, sc, sh).astype(_BF16)
        h_ref[tm:, :] = _halo_rows(i, n_tiles, xn_ref, xp_ref, n2, sc, sh).astype(_BF16)
        o_ref[...] = xm_ref[...]

    @pl.when((j == 1) & (i + 1 < n_tiles))
    def _():
        x_copy(i + 1).start()

    hb = h_ref[...]

    def conv(w_ref, cw_ref):
        u = jnp.dot(hb, w_ref[...], preferred_element_type=_F32)
        up, uc, un = _shift_rows(u, tm)
        return up * cw_ref[0:1, :] + uc * cw_ref[1:2, :] + un * cw_ref[2:3, :]

    half_g = 0.5 * conv(wg_ref, cwg_ref)
    silu_g = half_g + half_g * jnp.tanh(half_g)
    act = silu_g * conv(wa_ref, cwa_ref)
    o_ref[...] += g_ref[0:1, :] * jnp.dot(act.astype(_BF16), wd_ref[...], preferred_element_type=_F32)

    @pl.when(j == pl.num_programs(1) - 1)
    def _():
        o_ref[...] = _rms(o_ref[...], fn_ref[...])


def _conv_ffn(xs, norm2_w, mod, w_up_b, ffn_conv_w, w_down_b, final_w):
    n, d = xs.shape
    d_ff = w_down_b.shape[0]
    tm, tf = TM_FFN, TF_FFN
    assert n % tm == 0 and d_ff % tf == 0 and tm % HALO == 0
    nf = d_ff // tf
    assert nf >= 2, "the next tile's rows are prefetched at j == 1"
    r = tm // HALO
    last_halo = n // HALO - 1
    vec = pl.BlockSpec((1, d), lambda i, j: (0, 0))
    modspec = lambda k: pl.BlockSpec((SUBLANES_F32, d), lambda i, j: (0, k))
    return pl.pallas_call(
        _ffn_kernel,
        grid=(n // tm, nf),
        in_specs=[pl.BlockSpec((HALO, d), lambda i, j: (jnp.maximum(i * r - 1, 0), 0)),
                  pl.BlockSpec(memory_space=pl.ANY),
                  pl.BlockSpec((HALO, d), lambda i, j: (jnp.minimum((i + 1) * r, last_halo), 0)),
                  vec, modspec(3), modspec(4), modspec(5),
                  pl.BlockSpec((d, tf), lambda i, j: (0, j)),
                  pl.BlockSpec((d, tf), lambda i, j: (0, j + nf)),
                  pl.BlockSpec((CONV_K, tf), lambda i, j: (0, j)),
                  pl.BlockSpec((CONV_K, tf), lambda i, j: (0, j + nf)),
                  pl.BlockSpec((tf, d), lambda i, j: (j, 0)),
                  vec],
        out_specs=pl.BlockSpec((tm, d), lambda i, j: (i, 0)),
        out_shape=jax.ShapeDtypeStruct((n, d), _F32),
        scratch_shapes=[pltpu.VMEM((tm + 2 * HALO, d), _BF16),
                        pltpu.VMEM((tm, d), _F32),
                        pltpu.SemaphoreType.DMA(())],
        compiler_params=_params(2),
        name="conv_ffn",
    )(xs, xs, xs, norm2_w, mod, mod, mod, w_up_b, w_up_b, ffn_conv_w, ffn_conv_w, w_down_b, final_w)


def kernel(x, c, ctx, c_ctx, w_ada, b_ada, norm1_w, w_in, q_norm_w, k_norm_w, conv_w, attn_out_norm_w,
           conv_out_norm_w, w_o, norm2_w, w_ffn_up, ffn_conv_w, w_ffn_down, final_norm_w):
    b, n, d = x.shape
    assert b == 1 and w_ada.shape[0] == 1, "single batch element, single trunk layer"
    x2d = x[0]
    row = lambda v: v.reshape(1, -1)

    mod = _adaln(c, c_ctx, w_ada[0], b_ada[0])
    cos, sin = _rope_tables(max(n // GRID_W, GRID_W))

    w_in_b = w_in[0].astype(_BF16)
    qn_scaled = row(q_norm_w[0]) * (HEAD_DIM ** -0.5 * math.log2(math.e))
    kc, vc = _ctx_kv(ctx[0], row(norm1_w[0]), mod, w_in_b, row(k_norm_w[0]))
    q, k, v, conv_n = _in_proj(x2d, row(norm1_w[0]), mod, w_in_b, qn_scaled, row(k_norm_w[0]),
                               cos, sin, conv_w[0], row(conv_out_norm_w[0]))
    attn, (w_o_b, w_up_b, w_down_b) = _attention(
        q, kc, vc, k, v, [w_o, w_ffn_up, w_ffn_down])
    xs = _merge(attn, conv_n, row(attn_out_norm_w[0]), w_o_b, x2d, mod)
    out = _conv_ffn(xs, row(norm2_w[0]), mod, w_up_b, ffn_conv_w[0], w_down_b, row(final_norm_w))
    return out[None]
```
